```python
import jax, jax.numpy as jnp
from jax import lax
import numpy as np

D_MODEL = 2048
BATCH = 4
SEQ = 4096
DEPTH = 4

GRID_W = 64
CTX_LEN = 256
NORM_EPS = 1e-6
MLA_HEADS = 8
MLA_NOPE = 128
MLA_ROPE = 64
MLA_V = 128
MLA_Q_RANK = 512
MLA_KV_RANK = 256
ROPE_BASE = 10000.0
Q_BLOCK = 128
CONV_CH = 1024
CONV_GROUPS = 8
CONV_WIDTH = 31
GMLP_CH = 2048
GMLP_GROUPS = 8
CHUNK = 128
N_EXPERTS = 16
EXPERT_FF = 1024
CAPACITY_FACTOR = 2
KV_OFF = MLA_Q_RANK
KR_OFF = KV_OFF + MLA_KV_RANK
CONV_OFF = KR_OFF + MLA_ROPE
W_IN_EVEN = CONV_OFF + 2 * CONV_CH
MIX_EVEN = MLA_HEADS * MLA_V + CONV_CH
W_IN_ODD = 2 * GMLP_CH

kernel_name = 'hybrid_mla_conformer_gmlp_ecmoe_prefix_dit'


def rms_norm(x, g):
    xf = x.astype(jnp.float32)
    y = xf * lax.rsqrt(jnp.mean(xf * xf, axis=-1, keepdims=True) + NORM_EPS)
    return (y * g.astype(jnp.float32)).astype(x.dtype)


def layer_norm(x, g, b):
    xf = x.astype(jnp.float32)
    mu = jnp.mean(xf, axis=-1, keepdims=True)
    var = jnp.mean(jnp.square(xf - mu), axis=-1, keepdims=True)
    y = (xf - mu) * lax.rsqrt(var + NORM_EPS)
    return (y * g.astype(jnp.float32) + b.astype(jnp.float32)).astype(x.dtype)


def axial_rope(n_tok, dtype):
    rows = n_tok // GRID_W
    row, col = jnp.meshgrid(jnp.arange(rows, dtype=jnp.float32), jnp.arange(GRID_W, dtype=jnp.float32), indexing='ij')
    row, col = row.reshape(-1), col.reshape(-1)
    n_freq = MLA_ROPE // 4
    inv_freq = ROPE_BASE ** (-jnp.arange(n_freq, dtype=jnp.float32) / n_freq)
    ang = jnp.concatenate([row[:, None] * inv_freq, col[:, None] * inv_freq], axis=-1)
    return jnp.cos(ang).astype(dtype), jnp.sin(ang).astype(dtype)


def apply_rope(x, cos, sin):
    half = x.shape[-1] // 2
    x1, x2 = x[..., :half], x[..., half:]
    return jnp.concatenate([x1 * cos - x2 * sin, x2 * cos + x1 * sin], axis=-1)


def mla_kv(p_kv, kv_g, w_ukv, rope):
    B, T = p_kv.shape[:2]
    ckv = rms_norm(p_kv[..., :MLA_KV_RANK], kv_g)
    kv = (ckv @ w_ukv).reshape(B, T, MLA_HEADS, MLA_NOPE + MLA_V)
    k_nope, v = kv[..., :MLA_NOPE], kv[..., MLA_NOPE:]
    k_rope = p_kv[..., MLA_KV_RANK:]
    if rope is not None:
        k_rope = apply_rope(k_rope, rope[0][None], rope[1][None])
    k_rope = jnp.broadcast_to(k_rope[:, :, None, :], (B, T, MLA_HEADS, MLA_ROPE))
    return jnp.concatenate([k_nope, k_rope], axis=-1), v


def mla_q(p_q, q_g, w_uq, rope):
    B, T = p_q.shape[:2]
    q = (rms_norm(p_q, q_g) @ w_uq).reshape(B, T, MLA_HEADS, MLA_NOPE + MLA_ROPE)
    q_nope, q_rope = q[..., :MLA_NOPE], q[..., MLA_NOPE:]
    if rope is not None:
        q_rope = apply_rope(q_rope, rope[0][None, :, None, :], rope[1][None, :, None, :])
    return jnp.concatenate([q_nope, q_rope], axis=-1)


def softmax_attend(q, k, v):
    s = jnp.einsum('bqhd,bkhd->bhqk', q, k).astype(jnp.float32) * ((MLA_NOPE + MLA_ROPE) ** -0.5)
    p = jax.nn.softmax(s, axis=-1).astype(v.dtype)
    return jnp.einsum('bhqk,bkhd->bqhd', p, v)


def blocked_attend(q, k, v):
    B, T, H, dk = q.shape
    nb = T // Q_BLOCK
    qb = q.reshape(B, nb, Q_BLOCK, H, dk).transpose(1, 0, 2, 3, 4)
    ob = lax.map(lambda qq: softmax_attend(qq, k, v), qb)
    return ob.transpose(1, 0, 2, 3, 4).reshape(B, T, H * v.shape[-1])


def conformer_conv(p_glu, conv_w, conv_b, gn_g, gn_b):
    u = p_glu[..., :CONV_CH] * jax.nn.sigmoid(p_glu[..., CONV_CH:])
    pad = CONV_WIDTH // 2
    u = lax.conv_general_dilated(u, conv_w[:, None, :], (1,), [(pad, pad)],
                                 dimension_numbers=('NWC', 'WIO', 'NWC'),
                                 feature_group_count=CONV_CH) + conv_b
    B, T, _ = u.shape
    ug = u.reshape(B, T, CONV_GROUPS, CONV_CH // CONV_GROUPS).astype(jnp.float32)
    mu = jnp.mean(ug, axis=-1, keepdims=True)
    var = jnp.mean(jnp.square(ug - mu), axis=-1, keepdims=True)
    un = ((ug - mu) * lax.rsqrt(var + NORM_EPS)).reshape(B, T, CONV_CH)
    un = (un * gn_g.astype(jnp.float32) + gn_b.astype(jnp.float32)).astype(u.dtype)
    return jax.nn.silu(un)


def chunk_spatial_gate(p, ln_g, ln_b, w_s, b_s):
    B, T, _ = p.shape
    z = jax.nn.gelu(p)
    u, v = z[..., :GMLP_CH], z[..., GMLP_CH:]
    v = layer_norm(v, ln_g, ln_b)
    vc = v.reshape(B, T // CHUNK, CHUNK, GMLP_GROUPS, GMLP_CH // GMLP_GROUPS)
    s = jnp.einsum('gpq,bnqgc->bnpgc', w_s, vc) + b_s.T[None, None, :, :, None]
    return u * s.reshape(B, T, GMLP_CH)


def even_mixer(h_lat, h_ctx, w_in, q_g, kv_g, w_uq, w_ukv, conv_w, conv_b, gn_g, gn_b, w_out, need_ctx):
    rope = axial_rope(h_lat.shape[1], h_lat.dtype)
    p_lat = h_lat @ w_in
    k_lat, v_lat = mla_kv(p_lat[..., KV_OFF:CONV_OFF], kv_g, w_ukv, rope)
    k_ctx, v_ctx = mla_kv(h_ctx @ w_in[:, KV_OFF:CONV_OFF], kv_g, w_ukv, None)
    q_lat = mla_q(p_lat[..., :KV_OFF], q_g, w_uq, rope)
    k_all = jnp.concatenate([k_ctx, k_lat], axis=1)
    v_all = jnp.concatenate([v_ctx, v_lat], axis=1)
    a_lat = blocked_attend(q_lat, k_all, v_all)
    c_lat = conformer_conv(p_lat[..., CONV_OFF:], conv_w, conv_b, gn_g, gn_b)
    y_lat = jnp.concatenate([a_lat, c_lat], axis=-1) @ w_out
    if not need_ctx:
        return y_lat, None
    Bc, Tc = h_ctx.shape[:2]
    q_ctx = mla_q(h_ctx @ w_in[:, :KV_OFF], q_g, w_uq, None)
    a_ctx = softmax_attend(q_ctx, k_ctx, v_ctx).reshape(Bc, Tc, MLA_HEADS * MLA_V)
    c_ctx_out = conformer_conv(h_ctx @ w_in[:, CONV_OFF:], conv_w, conv_b, gn_g, gn_b)
    y_ctx = jnp.concatenate([a_ctx, c_ctx_out], axis=-1) @ w_out
    return y_lat, y_ctx


def odd_mixer(h_lat, h_ctx, w_in, ln_g, ln_b, w_s, b_s, w_out, need_ctx):
    y_lat = chunk_spatial_gate(h_lat @ w_in, ln_g, ln_b, w_s, b_s) @ w_out
    if not need_ctx:
        return y_lat, None
    y_ctx = chunk_spatial_gate(h_ctx @ w_in, ln_g, ln_b, w_s, b_s) @ w_out
    return y_lat, y_ctx


def expert_choice_ffn(h, w_r, w1, w3, w2):
    B, N, D = h.shape
    cap = CAPACITY_FACTOR * N // N_EXPERTS
    aff = jax.nn.softmax((h @ w_r).astype(jnp.float32), axis=-1)
    g, idx = lax.top_k(jnp.swapaxes(aff, 1, 2), cap)
    xs = jax.vmap(lambda hb, ib: hb[ib])(h, idx)
    hid = jax.nn.silu(jnp.einsum('becd,edf->becf', xs, w1)) * jnp.einsum('becd,edf->becf', xs, w3)
    y = jnp.einsum('becf,efd->becd', hid, w2) * g[..., None].astype(h.dtype)
    return jax.vmap(lambda ib, yb: jnp.zeros((N, D), h.dtype).at[ib.reshape(-1)].add(yb.reshape(-1, D)))(idx, y)


def setup_inputs(seed: int = 0) -> dict:
    key = jax.random.key(seed)
    ks = iter(jax.random.split(key, 40))
    n_even = (DEPTH + 1) // 2
    n_odd = DEPTH // 2
    D = D_MODEL

    def nrm(shape, scale):
        return jax.random.normal(next(ks), shape, jnp.float32) * scale

    def gain(shape):
        return 1.0 + nrm(shape, 0.02)

    return {
        'x': nrm((BATCH, SEQ, D), 1.0),
        'c': nrm((BATCH, D), 1.0),
        'ctx': nrm((BATCH, CTX_LEN, D), 1.0),
        'c_ctx': nrm((D,), 1.0),
        'w_mod': nrm((DEPTH, D, 6 * D), 0.5 * D ** -0.5),
        'b_mod': nrm((DEPTH, 6 * D), 0.02),
        'norm1_g': gain((DEPTH, D)),
        'norm2_g': gain((DEPTH, D)),
        'ev_w_in': nrm((n_even, D, W_IN_EVEN), D ** -0.5),
        'ev_q_g': gain((n_even, MLA_Q_RANK)),
        'ev_kv_g': gain((n_even, MLA_KV_RANK)),
        'ev_w_uq': nrm((n_even, MLA_Q_RANK, MLA_HEADS * (MLA_NOPE + MLA_ROPE)), MLA_Q_RANK ** -0.5),
        'ev_w_ukv': nrm((n_even, MLA_KV_RANK, MLA_HEADS * (MLA_NOPE + MLA_V)), MLA_KV_RANK ** -0.5),
        'ev_conv_w': nrm((n_even, CONV_WIDTH, CONV_CH), CONV_WIDTH ** -0.5),
        'ev_conv_b': nrm((n_even, CONV_CH), 0.02),
        'ev_gn_g': gain((n_even, CONV_CH)),
        'ev_gn_b': nrm((n_even, CONV_CH), 0.02),
        'ev_w_out': nrm((n_even, MIX_EVEN, D), MIX_EVEN ** -0.5),
        'od_w_in': nrm((n_odd, D, W_IN_ODD), D ** -0.5),
        'od_ln_g': gain((n_odd, GMLP_CH)),
        'od_ln_b': nrm((n_odd, GMLP_CH), 0.02),
        'od_w_s': nrm((n_odd, GMLP_GROUPS, CHUNK, CHUNK), CHUNK ** -0.5),
        'od_b_s': nrm((n_odd, GMLP_GROUPS, CHUNK), 0.02),
        'od_w_out': nrm((n_odd, GMLP_CH, D), GMLP_CH ** -0.5),
        'moe_w_r': nrm((DEPTH, D, N_EXPERTS), D ** -0.5),
        'moe_w1': nrm((DEPTH, N_EXPERTS, D, EXPERT_FF), D ** -0.5),
        'moe_w3': nrm((DEPTH, N_EXPERTS, D, EXPERT_FF), D ** -0.5),
        'moe_w2': nrm((DEPTH, N_EXPERTS, EXPERT_FF, D), EXPERT_FF ** -0.5),
        'final_g': gain((D,)),
    }


def reference(x, c, ctx, c_ctx, w_mod, b_mod, norm1_g, norm2_g,
              ev_w_in, ev_q_g, ev_kv_g, ev_w_uq, ev_w_ukv, ev_conv_w, ev_conv_b, ev_gn_g, ev_gn_b, ev_w_out,
              od_w_in, od_ln_g, od_ln_b, od_w_s, od_b_s, od_w_out,
              moe_w_r, moe_w1, moe_w3, moe_w2, final_g):
    for layer in range(DEPTH):
        need_ctx = layer < DEPTH - 1
        i = layer // 2
        mod_lat = jax.nn.silu(c) @ w_mod[layer] + b_mod[layer]
        mod_ctx = jax.nn.silu(c_ctx) @ w_mod[layer] + b_mod[layer]
        sh1, sc1, g1, sh2, sc2, g2 = [m[:, None, :] for m in jnp.split(mod_lat, 6, axis=-1)]
        csh1, csc1, cg1, csh2, csc2, cg2 = jnp.split(mod_ctx, 6, axis=-1)

        h_lat = rms_norm(x, norm1_g[layer]) * (1.0 + sc1) + sh1
        h_ctx = rms_norm(ctx, norm1_g[layer]) * (1.0 + csc1) + csh1
        if layer % 2 == 0:
            y_lat, y_ctx = even_mixer(h_lat, h_ctx, ev_w_in[i], ev_q_g[i], ev_kv_g[i], ev_w_uq[i], ev_w_ukv[i],
                                      ev_conv_w[i], ev_conv_b[i], ev_gn_g[i], ev_gn_b[i], ev_w_out[i], need_ctx)
        else:
            y_lat, y_ctx = odd_mixer(h_lat, h_ctx, od_w_in[i], od_ln_g[i], od_ln_b[i], od_w_s[i], od_b_s[i],
                                     od_w_out[i], need_ctx)
        x = x + g1 * y_lat

        h2 = rms_norm(x, norm2_g[layer]) * (1.0 + sc2) + sh2
        x = x + g2 * expert_choice_ffn(h2, moe_w_r[layer], moe_w1[layer], moe_w3[layer], moe_w2[layer])
        if need_ctx:
            ctx = ctx + cg1 * y_ctx
            hc2 = rms_norm(ctx, norm2_g[layer]) * (1.0 + csc2) + csh2
            ctx = ctx + cg2 * expert_choice_ffn(hc2, moe_w_r[layer], moe_w1[layer], moe_w3[layer], moe_w2[layer])
    return rms_norm(x, final_g)
```

```python
import functools
import math

import jax
import jax.numpy as jnp
from jax import lax
from jax.experimental import pallas as pl
from jax.experimental.pallas import tpu as pltpu

F32 = jnp.float32
BF16 = jnp.bfloat16
I32 = jnp.int32

NORM_EPS = 1e-6
GRID_W = 64
ROPE_BASE = 10000.0
HEADS = 8
NOPE = 128
ROPE = 64
VDIM = 128
Q_RANK = 512
KV_RANK = 256
CONV_CH = 1024
CONV_GROUPS = 8
CONV_WIDTH = 31
GMLP_CH = 2048
GMLP_GROUPS = 8
CHUNK = 128
N_EXPERTS = 16
CAPACITY_FACTOR = 2

LANES = 128
TILE = 256
HALO = 16
HEAD_SLOT = 256
N_MOD = 6
CTX_MOD_ROW = 4
MOD_ROWS = 8
IN_EVEN = 3072
SEL_ROWS = 8


def _cparams(sem, vmem_mb):
    return pltpu.CompilerParams(dimension_semantics=sem, vmem_limit_bytes=vmem_mb << 20)


def _resident(shape):
    nd = len(shape)
    return pl.BlockSpec(shape, lambda *_: (0,) * nd, pipeline_mode=pl.Buffered(1))


def _rms(x, g):
    return x * lax.rsqrt(jnp.mean(x * x, axis=-1, keepdims=True) + NORM_EPS) * g


def _mod_spec(k, n_lat_tiles, d):
    return pl.BlockSpec((1, 1, d), lambda b, i: (jnp.where(i >= n_lat_tiles, CTX_MOD_ROW, b) * N_MOD + k, 0, 0))


def _mod_body(c_ref, w_ref, b_ref, o_ref):
    a = c_ref[...]
    a = a * jax.nn.sigmoid(a)
    o_ref[0] = jnp.dot(a.astype(BF16), w_ref[0].astype(BF16), preferred_element_type=F32) + b_ref[0]


def _modulation(cc, w_mod, b_mod):
    n_layers, d, n = w_mod.shape
    tn = max(t for t in (1024, 512, 256, LANES) if n % t == 0)
    return pl.pallas_call(
        _mod_body,
        grid=(n_layers, n // tn),
        in_specs=[pl.BlockSpec((MOD_ROWS, d), lambda l, j: (0, 0)),
                  pl.BlockSpec((1, d, tn), lambda l, j: (l, 0, j)),
                  pl.BlockSpec((1, 1, tn), lambda l, j: (l, 0, j))],
        out_specs=pl.BlockSpec((1, MOD_ROWS, tn), lambda l, j: (l, 0, j)),
        out_shape=jax.ShapeDtypeStruct((n_layers, MOD_ROWS, n), F32),
        compiler_params=_cparams(("arbitrary", "arbitrary"), 40),
        name="modulation",
    )(cc, w_mod, b_mod.reshape(n_layers, 1, n))


def _rope128(grp, c, sa, sb):
    return grp * c + pltpu.roll(grp, 96, 1) * sa + pltpu.roll(grp, 32, 1) * sb


def _even_in_body(x_ref, sh_ref, sc_ref, g_ref, w_ref, qg_ref, kvg_ref, wuq_ref, wuk_ref, wuvt_ref,
                  rc_ref, rsa_ref, rsb_ref, pc_ref, q_ref, k_ref, vt_ref):
    h = _rms(x_ref[0], g_ref[...]) * (1.0 + sc_ref[0]) + sh_ref[0]
    p = jnp.dot(h.astype(BF16), w_ref[...], preferred_element_type=F32)
    n_conv = 2 * CONV_CH
    pc_ref[0] = p[:, :n_conv].astype(BF16)
    pq = p[:, n_conv:n_conv + Q_RANK]
    pkv = p[:, n_conv + Q_RANK:n_conv + Q_RANK + KV_RANK]
    pkr = p[:, n_conv + Q_RANK + KV_RANK:n_conv + Q_RANK + KV_RANK + LANES]
    rc, rsa, rsb = rc_ref[...], rsa_ref[...], rsb_ref[...]
    q = jnp.dot(_rms(pq, qg_ref[...]).astype(BF16), wuq_ref[...], preferred_element_type=F32)
    ckv = _rms(pkv, kvg_ref[...]).astype(BF16)
    kn = jnp.dot(ckv, wuk_ref[...], preferred_element_type=F32)
    kr = _rope128(pkr, rc, rsa, rsb).astype(BF16)
    vt = lax.dot_general(wuvt_ref[...], ckv, (((1,), (1,)), ((), ())), preferred_element_type=F32)
    for hd in range(HEADS):
        o = hd * HEAD_SLOT
        q_ref[0, :, o:o + NOPE] = q[:, o:o + NOPE].astype(BF16)
        q_ref[0, :, o + NOPE:o + HEAD_SLOT] = _rope128(q[:, o + NOPE:o + HEAD_SLOT], rc, rsa, rsb).astype(BF16)
        k_ref[0, :, o:o + NOPE] = kn[:, hd * NOPE:(hd + 1) * NOPE].astype(BF16)
        k_ref[0, :, o + NOPE:o + HEAD_SLOT] = kr
        vt_ref[0, hd, 0] = vt[hd * VDIM:(hd + 1) * VDIM, :].astype(BF16)


def _even_in(x, modl, g, w, qg, kvg, wuq, wuk, wuvt, rc, rsa, rsb, n_lat_tiles):
    b, s, d = x.shape
    nt = s // TILE
    tile = lambda n: pl.BlockSpec((1, TILE, n), lambda bb, i: (bb, i, 0))
    rope = pl.BlockSpec((TILE, LANES), lambda bb, i: (i, 0))
    return pl.pallas_call(
        _even_in_body,
        grid=(b, nt),
        in_specs=[tile(d), _mod_spec(0, n_lat_tiles, d), _mod_spec(1, n_lat_tiles, d), _resident((1, d)),
                  _resident(w.shape), _resident(qg.shape), _resident(kvg.shape), _resident(wuq.shape),
                  _resident(wuk.shape), _resident(wuvt.shape), rope, rope, rope],
        out_specs=[tile(2 * CONV_CH), tile(HEADS * HEAD_SLOT), tile(HEADS * HEAD_SLOT),
                   pl.BlockSpec((1, HEADS, 1, VDIM, TILE), lambda bb, i: (bb, 0, i, 0, 0))],
        out_shape=[jax.ShapeDtypeStruct((b, s, 2 * CONV_CH), BF16),
                   jax.ShapeDtypeStruct((b, s, HEADS * HEAD_SLOT), BF16),
                   jax.ShapeDtypeStruct((b, s, HEADS * HEAD_SLOT), BF16),
                   jax.ShapeDtypeStruct((b, HEADS, nt, VDIM, TILE), BF16)],
        compiler_params=_cparams(("arbitrary", "arbitrary"), 48),
        name="even_in",
    )(x, modl, modl, g, w, qg, kvg, wuq, wuk, wuvt, rc, rsa, rsb)


def _glu(v):
    return v[:, :CONV_CH].astype(F32) * jax.nn.sigmoid(v[:, CONV_CH:].astype(F32))


def _conv_body(cur_ref, prev_ref, next_ref, w_ref, b_ref, gg_ref, gb_ref, o_ref, ubuf, *, n_lat_tiles, n_tiles):
    i = pl.program_id(1)
    left_ok = jnp.logical_and(i != 0, i != n_lat_tiles)
    right_ok = jnp.logical_and(i != n_lat_tiles - 1, i != n_tiles - 1)
    ubuf[0:HALO, :] = jnp.where(left_ok, _glu(prev_ref[0]), 0.0)
    ubuf[HALO:HALO + TILE, :] = _glu(cur_ref[0])
    ubuf[HALO + TILE:2 * HALO + TILE, :] = jnp.where(right_ok, _glu(next_ref[0]), 0.0)
    pad = CONV_WIDTH // 2
    gsz = CONV_CH // CONV_GROUPS
    rows = 64
    for grp in range(CONV_GROUPS):
        ls = slice(grp * gsz, (grp + 1) * gsz)
        for r0 in range(0, TILE, rows):
            acc = jnp.zeros((rows, gsz), F32)
            for kk in range(CONV_WIDTH):
                o = HALO - pad + kk + r0
                acc = acc + w_ref[kk:kk + 1, ls] * ubuf[o:o + rows, ls]
            y = acc + b_ref[:, ls]
            mu = jnp.mean(y, axis=-1, keepdims=True)
            yc = y - mu
            var = jnp.mean(yc * yc, axis=-1, keepdims=True)
            yn = yc * lax.rsqrt(var + NORM_EPS) * gg_ref[:, ls] + gb_ref[:, ls]
            o_ref[0, r0:r0 + rows, ls] = (yn * jax.nn.sigmoid(yn)).astype(BF16)


def _conv_branch(pc, w, bias, gg, gb, n_lat_tiles):
    b, s, _ = pc.shape
    nt = s // TILE
    hb = TILE // HALO
    nh = s // HALO
    return pl.pallas_call(
        functools.partial(_conv_body, n_lat_tiles=n_lat_tiles, n_tiles=nt),
        grid=(b, nt),
        in_specs=[pl.BlockSpec((1, TILE, 2 * CONV_CH), lambda bb, i: (bb, i, 0)),
                  pl.BlockSpec((1, HALO, 2 * CONV_CH), lambda bb, i: (bb, jnp.maximum(i * hb - 1, 0), 0)),
                  pl.BlockSpec((1, HALO, 2 * CONV_CH), lambda bb, i: (bb, jnp.minimum((i + 1) * hb, nh - 1), 0)),
                  _resident(w.shape), _resident(bias.shape), _resident(gg.shape), _resident(gb.shape)],
        out_specs=pl.BlockSpec((1, TILE, CONV_CH), lambda bb, i: (bb, i, 0)),
        out_shape=jax.ShapeDtypeStruct((b, s, CONV_CH), BF16),
        scratch_shapes=[pltpu.VMEM((TILE + 2 * HALO, CONV_CH), F32)],
        compiler_params=_cparams(("arbitrary", "arbitrary"), 32),
        name="conv_branch",
    )(pc, pc, pc, w, bias, gg, gb)


def _attn_body(q_ref, k_ref, v_ref, o_ref, m_s, l_s, acc_s, *, n_chunks, coef):
    q = q_ref[0]
    m_s[...] = jnp.full(m_s.shape, -jnp.inf, F32)
    l_s[...] = jnp.zeros(l_s.shape, F32)
    acc_s[...] = jnp.zeros(acc_s.shape, F32)

    def step(j, carry):
        kc = k_ref[0, pl.ds(pl.multiple_of(j * TILE, TILE), TILE), :]
        st = lax.dot_general(kc, q, (((1,), (1,)), ((), ())), preferred_element_type=F32)
        m_old = m_s[...]
        m_new = jnp.maximum(m_old, jnp.max(st, axis=0, keepdims=True))
        alpha = jnp.exp2((m_old - m_new) * coef)
        p = jnp.exp2((st - m_new) * coef)
        l_s[...] = alpha * l_s[...] + jnp.sum(p, axis=0, keepdims=True)
        acc_s[...] = alpha * acc_s[...] + jnp.dot(v_ref[0, 0, j], p.astype(BF16), preferred_element_type=F32)
        m_s[...] = m_new
        return carry

    lax.fori_loop(0, n_chunks, step, 0)
    o_ref[0] = (acc_s[...] / l_s[...]).T.astype(BF16)


def _attention(q, k, vt, *, tq, q_tile0, n_q_tiles, kv_row0, n_chunks):
    b, s, _ = q.shape
    kv_rows = n_chunks * TILE
    kb = kv_row0 // kv_rows
    cb = kv_row0 // TILE // n_chunks
    coef = (NOPE + ROPE) ** -0.5 * math.log2(math.e)
    return pl.pallas_call(
        functools.partial(_attn_body, n_chunks=n_chunks, coef=coef),
        grid=(b, HEADS, n_q_tiles),
        in_specs=[pl.BlockSpec((1, tq, HEAD_SLOT), lambda bb, h, i: (bb, q_tile0 + i, h)),
                  pl.BlockSpec((1, kv_rows, HEAD_SLOT), lambda bb, h, i: (bb, kb, h)),
                  pl.BlockSpec((1, 1, n_chunks, VDIM, TILE), lambda bb, h, i: (bb, h, cb, 0, 0))],
        out_specs=pl.BlockSpec((1, tq, VDIM), lambda bb, h, i: (bb, i, h)),
        out_shape=jax.ShapeDtypeStruct((b, n_q_tiles * tq, HEADS * VDIM), BF16),
        scratch_shapes=[pltpu.VMEM((1, tq), F32), pltpu.VMEM((1, tq), F32), pltpu.VMEM((VDIM, tq), F32)],
        compiler_params=_cparams(("arbitrary", "arbitrary", "arbitrary"), 32),
        name="attention",
    )(q, k, vt)


def _gelu_tanh(x):
    return 0.5 * x * (1.0 + jnp.tanh(math.sqrt(2.0 / math.pi) * (x + 0.044715 * (x * x * x))))


def _odd_in_body(x_ref, sh_ref, sc_ref, g_ref, w_ref, lg_ref, lb_ref, ws_ref, bst_ref, o_ref):
    h = _rms(x_ref[0], g_ref[...]) * (1.0 + sc_ref[0]) + sh_ref[0]
    z = _gelu_tanh(jnp.dot(h.astype(BF16), w_ref[...], preferred_element_type=F32))
    u = z[:, :GMLP_CH]
    v = z[:, GMLP_CH:]
    mu = jnp.mean(v, axis=-1, keepdims=True)
    vc = v - mu
    var = jnp.mean(vc * vc, axis=-1, keepdims=True)
    vn = (vc * lax.rsqrt(var + NORM_EPS) * lg_ref[...] + lb_ref[...]).astype(BF16)
    gsz = GMLP_CH // GMLP_GROUPS
    for c0 in range(0, TILE, CHUNK):
        for grp in range(GMLP_GROUPS):
            ls = slice(grp * gsz, (grp + 1) * gsz)
            sg = jnp.dot(ws_ref[grp], vn[c0:c0 + CHUNK, ls], preferred_element_type=F32) + bst_ref[:, grp:grp + 1]
            o_ref[0, c0:c0 + CHUNK, ls] = (u[c0:c0 + CHUNK, ls] * sg).astype(BF16)


def _odd_in(x, modl, g, w, lg, lb, ws, bst, n_lat_tiles):
    b, s, d = x.shape
    nt = s // TILE
    return pl.pallas_call(
        _odd_in_body,
        grid=(b, nt),
        in_specs=[pl.BlockSpec((1, TILE, d), lambda bb, i: (bb, i, 0)),
                  _mod_spec(0, n_lat_tiles, d), _mod_spec(1, n_lat_tiles, d), _resident((1, d)),
                  _resident(w.shape), _resident(lg.shape), _resident(lb.shape), _resident(ws.shape),
                  _resident(bst.shape)],
        out_specs=pl.BlockSpec((1, TILE, GMLP_CH), lambda bb, i: (bb, i, 0)),
        out_shape=jax.ShapeDtypeStruct((b, s, GMLP_CH), BF16),
        compiler_params=_cparams(("arbitrary", "arbitrary"), 56),
        name="odd_in",
    )(x, modl, modl, g, w, lg, lb, ws, bst)


def _out_body(*refs, n_parts):
    a_refs = refs[:n_parts]
    w_ref, x_ref, g1_ref, n2_ref, sh_ref, sc_ref, wrt_ref, xo_ref, aff_ref = refs[n_parts:]
    y = None
    off = 0
    for a_ref in a_refs:
        kk = a_ref.shape[-1]
        t = jnp.dot(a_ref[0], w_ref[off:off + kk, :], preferred_element_type=F32)
        y = t if y is None else y + t
        off += kk
    xn = x_ref[0] + g1_ref[0] * y
    xo_ref[0] = xn
    h2 = _rms(xn, n2_ref[...]) * (1.0 + sc_ref[0]) + sh_ref[0]
    lg = lax.dot_general(wrt_ref[...], h2.astype(BF16), (((1,), (1,)), ((), ())), preferred_element_type=F32)
    e = jnp.exp(lg - jnp.max(lg, axis=0, keepdims=True))
    aff_ref[0] = e / jnp.sum(e, axis=0, keepdims=True)


def _out_proj(parts, w, x, modl, n2, wrt, n_lat_tiles):
    b, s, d = x.shape
    nt = s // TILE
    tile = lambda n: pl.BlockSpec((1, TILE, n), lambda bb, i: (bb, i, 0))
    return pl.pallas_call(
        functools.partial(_out_body, n_parts=len(parts)),
        grid=(b, nt),
        in_specs=[tile(a.shape[-1]) for a in parts]
        + [_resident(w.shape), tile(d), _mod_spec(2, n_lat_tiles, d), _resident((1, d)),
           _mod_spec(3, n_lat_tiles, d), _mod_spec(4, n_lat_tiles, d), _resident(wrt.shape)],
        out_specs=[tile(d), pl.BlockSpec((1, N_EXPERTS, TILE), lambda bb, i: (bb, 0, i))],
        out_shape=[jax.ShapeDtypeStruct((b, s, d), F32), jax.ShapeDtypeStruct((b, N_EXPERTS, s), F32)],
        compiler_params=_cparams(("arbitrary", "arbitrary"), 40),
        name="out_proj",
    )(*parts, w, x, modl, n2, modl, modl, wrt)


def _count(pred):
    return jnp.sum(pred.astype(F32), axis=1, keepdims=True)


def _cumsum_lanes(x):
    r, n = x.shape
    tri = (lax.broadcasted_iota(I32, (LANES, LANES), 0) <= lax.broadcasted_iota(I32, (LANES, LANES), 1)).astype(BF16)
    outs = []
    off = jnp.zeros((r, 1), F32)
    for blk in range(n // LANES):
        loc = jnp.dot(x[:, blk * LANES:(blk + 1) * LANES].astype(BF16), tri, preferred_element_type=F32) + off
        outs.append(loc)
        off = loc[:, LANES - 1:LANES]
    return jnp.concatenate(outs, axis=1)


def _split3(x):
    hi = x.astype(BF16).astype(F32)
    r1 = x - hi
    mid = r1.astype(BF16).astype(F32)
    lo = (r1 - mid).astype(BF16).astype(F32)
    return hi, mid, lo


def _topk_body(aff_ref, sel_ref, sx_ref, tc_ref, mask_s, pm_s, planes_s, res_s, *, segs):
    s_total = aff_ref.shape[-1]
    for off, n, cap, _ in segs:
        a = aff_ref[0, :, off:off + n]
        bits = pltpu.bitcast(a, I32)
        thr = jnp.zeros((N_EXPERTS, 1), I32)
        for bit in range(30, -1, -1):
            cand = thr | (1 << bit)
            thr = jnp.where(_count(bits >= cand) >= cap, cand, thr)
        gt = bits > thr
        eq = bits == thr
        need = cap - _count(gt)
        idx = lax.broadcasted_iota(I32, (N_EXPERTS, n), 1)
        j0 = jnp.zeros((N_EXPERTS, 1), I32)
        for bit in range(n.bit_length() - 1, -1, -1):
            cand = j0 | (1 << bit)
            j0 = jnp.where(_count(jnp.logical_and(eq, idx < cand)) < need, cand, j0)
        m = jnp.logical_or(gt, jnp.logical_and(eq, idx <= j0)).astype(F32)
        mask_s[:, off:off + n] = m
        pm_s[:, off:off + n] = _cumsum_lanes(m) * m

    m_all = mask_s[...]
    tc = jnp.sum(m_all, axis=0, keepdims=True)
    sx = _cumsum_lanes(jnp.broadcast_to(tc, (8, s_total)))[0:1] - tc
    tc_ref[0] = tc
    sx_ref[0] = sx
    lower = (lax.broadcasted_iota(I32, (N_EXPERTS, N_EXPERTS), 1)
             < lax.broadcasted_iota(I32, (N_EXPERTS, N_EXPERTS), 0)).astype(BF16)
    rank = jnp.dot(lower, m_all.astype(BF16), preferred_element_type=F32)
    pos = sx + rank
    tok = lax.broadcasted_iota(I32, (N_EXPERTS, s_total), 1).astype(F32)
    g_hi, g_mid, g_lo = _split3(aff_ref[0])
    tok_hi = jnp.floor(tok * (1.0 / 256.0))
    pos_hi = jnp.floor(pos * (1.0 / 256.0))
    planes = (tok_hi, tok - 256.0 * tok_hi, pos_hi, pos - 256.0 * pos_hi, g_hi, g_mid, g_lo, jnp.zeros_like(tok))
    for r, pln in enumerate(planes):
        planes_s[r * N_EXPERTS:(r + 1) * N_EXPERTS, :] = pln.astype(BF16)

    def per_expert(e, carry):
        for off, n, cap, slot0 in segs:
            blk = min(cap, TILE)
            prow = pm_s[pl.ds(e, 1), off:off + n]
            vals = planes_s[:, off:off + n]
            for c0 in range(0, cap, blk):
                slot = (lax.broadcasted_iota(I32, (blk, n), 0) + (c0 + 1)).astype(F32)
                onehot = (prow == slot).astype(BF16)
                res_s[:, 0:blk] = lax.dot_general(vals, onehot, (((1,), (1,)), ((), ())),
                                                  preferred_element_type=F32)
                for r in range(SEL_ROWS):
                    row = res_s[pl.ds(r * N_EXPERTS + e, 1), :]
                    sel_ref[0, e, r:r + 1, slot0 + c0:slot0 + c0 + blk] = row[:, 0:blk]
        return carry

    lax.fori_loop(0, N_EXPERTS, per_expert, 0)


def _topk(aff_t, segs, capt):
    b, e, s = aff_t.shape
    return pl.pallas_call(
        functools.partial(_topk_body, segs=segs),
        grid=(b,),
        in_specs=[pl.BlockSpec((1, e, s), lambda bb: (bb, 0, 0))],
        out_specs=[pl.BlockSpec((1, e, SEL_ROWS, capt), lambda bb: (bb, 0, 0, 0)),
                   pl.BlockSpec((1, 1, s), lambda bb: (bb, 0, 0)),
                   pl.BlockSpec((1, 1, s), lambda bb: (bb, 0, 0))],
        out_shape=[jax.ShapeDtypeStruct((b, e, SEL_ROWS, capt), F32),
                   jax.ShapeDtypeStruct((b, 1, s), F32), jax.ShapeDtypeStruct((b, 1, s), F32)],
        scratch_shapes=[pltpu.VMEM((e, s), F32), pltpu.VMEM((e, s), F32), pltpu.VMEM((SEL_ROWS * e, s), BF16),
                        pltpu.VMEM((SEL_ROWS * e, TILE), F32)],
        compiler_params=_cparams(("arbitrary",), 48),
        name="topk",
    )(aff_t)


def _next_step(e, b, n_b):
    b1 = b + 1
    wrap = b1 == n_b
    return jnp.where(wrap, e + 1, e), jnp.where(wrap, 0, b1)


def _expert_body(ip_hbm, gate_ref, x_hbm, shb_ref, scb_ref, shc_ref, scc_ref, n2_ref, w1_ref, w3_ref, w2_ref,
                 z_hbm, ip_s, ip_sem, xs_buf, gsem, y_buf, ssem, *, n_b, cap_l, capt):
    e = pl.program_id(0)
    b = pl.program_id(1)
    n_e = pl.num_programs(0)
    s = e * n_b + b
    n_steps = n_e * n_b
    slot = lax.rem(s, 2)
    e1, b1 = _next_step(e, b, n_b)
    e2, b2 = _next_step(e1, b1, n_b)

    def ip_copy(ee, bb, sl):
        return pltpu.make_async_copy(ip_hbm.at[pl.ds(bb * n_e + ee, 1)], ip_s.at[pl.ds(sl, 1)], ip_sem.at[sl])

    def gather_start(bb, sl):
        def body(c, carry):
            r = ip_s[sl, c]
            pltpu.make_async_copy(x_hbm.at[bb, pl.ds(r, 1)], xs_buf.at[sl, pl.ds(c, 1)], gsem.at[sl]).start()
            return carry
        lax.fori_loop(0, capt, body, 0, unroll=8)

    def scatter_wait():
        pltpu.make_async_copy(y_buf, z_hbm.at[0, pl.ds(0, capt)], ssem).wait()

    @pl.when(s == 0)
    def _():
        ip_copy(e, b, 0).start()
        ip_copy(e, b, 0).wait()
        gather_start(b, 0)
        ip_copy(e1, b1, 1).start()

    @pl.when(s + 1 < n_steps)
    def _():
        ip_copy(e1, b1, 1 - slot).wait()
        gather_start(b1, 1 - slot)

    pltpu.make_async_copy(x_hbm.at[0, pl.ds(0, capt)], xs_buf.at[slot], gsem.at[slot]).wait()

    hn = _rms(xs_buf[slot], n2_ref[...])
    is_lat = lax.broadcasted_iota(I32, (capt, 1), 0) < cap_l
    sc = jnp.where(is_lat, scb_ref[0], scc_ref[0])
    sh = jnp.where(is_lat, shb_ref[0], shc_ref[0])
    hb = (hn * (1.0 + sc) + sh).astype(BF16)
    a1 = jnp.dot(hb, w1_ref[0], preferred_element_type=F32)
    a3 = jnp.dot(hb, w3_ref[0], preferred_element_type=F32)
    hid = (a1 * jax.nn.sigmoid(a1) * a3).astype(BF16)
    y = jnp.dot(hid, w2_ref[0], preferred_element_type=F32) * gate_ref[0]

    @pl.when(s > 0)
    def _():
        scatter_wait()

    y_buf[...] = y

    def scatter_body(c, carry):
        r = ip_s[slot, capt + c]
        pltpu.make_async_copy(y_buf.at[pl.ds(c, 1)], z_hbm.at[b, pl.ds(r, 1)], ssem).start()
        return carry
    lax.fori_loop(0, capt, scatter_body, 0, unroll=8)

    @pl.when(s + 2 < n_steps)
    def _():
        ip_copy(e2, b2, slot).start()

    @pl.when(s == n_steps - 1)
    def _():
        scatter_wait()


def _expert_ffn(ip, gate, x, modl, n2, w1, w3, w2, cap_l, capt):
    bsz, s, d = x.shape
    n_e, _, f = w1.shape
    ne = n_e * capt
    any_spec = pl.BlockSpec(memory_space=pl.ANY)
    mod_b = lambda k: pl.BlockSpec((1, 1, d), lambda e, b: (b * N_MOD + k, 0, 0))
    mod_c = lambda k: pl.BlockSpec((1, 1, d), lambda e, b: (CTX_MOD_ROW * N_MOD + k, 0, 0))
    return pl.pallas_call(
        functools.partial(_expert_body, n_b=bsz, cap_l=cap_l, capt=capt),
        grid=(n_e, bsz),
        in_specs=[any_spec,
                  pl.BlockSpec((1, capt, 1), lambda e, b: (b * n_e + e, 0, 0)),
                  any_spec, mod_b(3), mod_b(4), mod_c(3), mod_c(4), _resident((1, d)),
                  pl.BlockSpec((1, d, f), lambda e, b: (e, 0, 0)),
                  pl.BlockSpec((1, d, f), lambda e, b: (e, 0, 0)),
                  pl.BlockSpec((1, f, d), lambda e, b: (e, 0, 0))],
        out_specs=any_spec,
        out_shape=jax.ShapeDtypeStruct((bsz, ne, d), F32),
        scratch_shapes=[pltpu.SMEM((2, 2 * capt), I32), pltpu.SemaphoreType.DMA((2,)),
                        pltpu.VMEM((2, capt, d), F32), pltpu.SemaphoreType.DMA((2,)),
                        pltpu.VMEM((capt, d), F32), pltpu.SemaphoreType.DMA(())],
        compiler_params=_cparams(("arbitrary", "arbitrary"), 60),
        name="expert_ffn",
    )(ip, gate, x, modl, modl, modl, modl, n2, w1, w3, w2)


def _combine_body(ts_ref, x_ref, sx_ref, tc_ref, g2_ref, z_hbm, o_ref, zbuf, zsem, acc_s, *, n_tiles, ne):
    b = pl.program_id(0)
    i = pl.program_id(1)
    e_lo = ts_ref[b * (n_tiles + 1) + i]
    e_hi = ts_ref[b * (n_tiles + 1) + i + 1]
    base = lax.shift_left(lax.shift_right_logical(e_lo, 3), 3)
    n_ch = lax.shift_right_logical(e_hi - base + (TILE - 1), TILE.bit_length() - 1)

    def chunk_start(k):
        return pl.multiple_of(jnp.minimum(base + k * TILE, ne - TILE), 8)

    def chunk_copy(k, sl):
        return pltpu.make_async_copy(z_hbm.at[b, pl.ds(chunk_start(k), TILE)], zbuf.at[sl], zsem.at[sl])

    acc_s[...] = jnp.zeros(acc_s.shape, F32)
    sx = sx_ref[0]
    end = sx + tc_ref[0]

    @pl.when(n_ch > 0)
    def _():
        chunk_copy(0, 0).start()

    def body(k, carry):
        sl = lax.rem(k, 2)

        @pl.when(k + 1 < n_ch)
        def _():
            chunk_copy(k + 1, 1 - sl).start()

        chunk_copy(k, sl).wait()
        ids = (chunk_start(k) + lax.broadcasted_iota(I32, (TILE, 1), 0))
        idf = ids.astype(F32)
        own = jnp.logical_and(jnp.logical_and(ids >= base + k * TILE, idf >= sx), idf < end)
        acc_s[...] += lax.dot_general(own.astype(BF16), zbuf[sl].astype(BF16), (((0,), (0,)), ((), ())),
                                      preferred_element_type=F32)
        return carry

    lax.fori_loop(0, n_ch, body, 0)
    o_ref[0] = x_ref[0] + g2_ref[0] * acc_s[...]


def _combine(ts, x, sx, tc, modl, z, n_lat_tiles):
    bsz, s, d = x.shape
    nt = s // TILE
    ne = z.shape[1]
    grid_spec = pltpu.PrefetchScalarGridSpec(
        num_scalar_prefetch=1,
        grid=(bsz, nt),
        in_specs=[pl.BlockSpec((1, TILE, d), lambda b, i, t: (b, i, 0)),
                  pl.BlockSpec((1, 1, TILE), lambda b, i, t: (b, 0, i)),
                  pl.BlockSpec((1, 1, TILE), lambda b, i, t: (b, 0, i)),
                  pl.BlockSpec((1, 1, d), lambda b, i, t: (jnp.where(i >= n_lat_tiles, CTX_MOD_ROW, b) * N_MOD + 5, 0, 0)),
                  pl.BlockSpec(memory_space=pl.ANY)],
        out_specs=pl.BlockSpec((1, TILE, d), lambda b, i, t: (b, i, 0)),
        scratch_shapes=[pltpu.VMEM((2, TILE, d), F32), pltpu.SemaphoreType.DMA((2,)), pltpu.VMEM((TILE, d), F32)],
    )
    return pl.pallas_call(
        functools.partial(_combine_body, n_tiles=nt, ne=ne),
        grid_spec=grid_spec,
        out_shape=jax.ShapeDtypeStruct((bsz, s, d), F32),
        compiler_params=_cparams(("arbitrary", "arbitrary"), 32),
        name="combine",
    )(ts, x, sx, tc, modl, z)


def _final_body(x_ref, g_ref, o_ref):
    o_ref[0] = _rms(x_ref[0], g_ref[...])


def _final_norm(x, g, n_lat):
    bsz, _, d = x.shape
    return pl.pallas_call(
        _final_body,
        grid=(bsz, n_lat // TILE),
        in_specs=[pl.BlockSpec((1, TILE, d), lambda b, i: (b, i, 0)), _resident((1, d))],
        out_specs=pl.BlockSpec((1, TILE, d), lambda b, i: (b, i, 0)),
        out_shape=jax.ShapeDtypeStruct((bsz, n_lat, d), F32),
        compiler_params=_cparams(("arbitrary", "arbitrary"), 32),
        name="final_norm",
    )(x, g)


def _rope_tables(n_lat, n_ctx):
    rows = n_lat // GRID_W
    row, col = jnp.meshgrid(jnp.arange(rows, dtype=F32), jnp.arange(GRID_W, dtype=F32), indexing="ij")
    row, col = row.reshape(-1), col.reshape(-1)
    n_freq = ROPE // 4
    inv_freq = ROPE_BASE ** (-jnp.arange(n_freq, dtype=F32) / n_freq)
    ang = jnp.concatenate([row[:, None] * inv_freq, col[:, None] * inv_freq], axis=-1)
    cos, sin = jnp.cos(ang), jnp.sin(ang)
    half = ROPE // 2
    zero = lambda n: jnp.zeros((n_lat, n), F32)
    rc = jnp.concatenate([cos, cos, jnp.ones((n_lat, LANES - ROPE), F32)], axis=1)
    rsa = jnp.concatenate([-sin, zero(LANES - half)], axis=1)
    rsb = jnp.concatenate([zero(half), sin, zero(LANES - ROPE)], axis=1)
    ident = jnp.ones((n_ctx, LANES), F32)
    none = jnp.zeros((n_ctx, LANES), F32)
    return (jnp.concatenate([rc, ident]), jnp.concatenate([rsa, none]), jnp.concatenate([rsb, none]))


def kernel(x, c, ctx, c_ctx, w_mod, b_mod, norm1_g, norm2_g, ev_w_in, ev_q_g, ev_kv_g, ev_w_uq, ev_w_ukv, ev_conv_w, ev_conv_b, ev_gn_g, ev_gn_b, ev_w_out, od_w_in, od_ln_g, od_ln_b, od_w_s, od_b_s, od_w_out, moe_w_r, moe_w1, moe_w3, moe_w2, final_g):
    bsz, n_lat, d = x.shape
    n_ctx = ctx.shape[1]
    depth = w_mod.shape[0]
    assert bsz <= CTX_MOD_ROW and n_lat % (2 * TILE) == 0 and n_ctx == TILE and d % LANES == 0
    n_lat_tiles = n_lat // TILE
    cap_l = CAPACITY_FACTOR * n_lat // N_EXPERTS
    cap_c = CAPACITY_FACTOR * n_ctx // N_EXPERTS
    capt = cap_l + cap_c
    segs = ((0, n_lat, cap_l, 0), (n_lat, n_ctx, cap_c, cap_l))

    cc = jnp.zeros((MOD_ROWS, d), F32).at[:bsz].set(c).at[CTX_MOD_ROW].set(c_ctx)
    mod = _modulation(cc, w_mod, b_mod)
    xs = jnp.concatenate([x, ctx], axis=1)
    rc, rsa, rsb = _rope_tables(n_lat, n_ctx)
    row2 = lambda v: v.reshape(1, -1)

    for layer in range(depth):
        i = layer // 2
        modl = mod[layer].reshape(MOD_ROWS * N_MOD, 1, d)
        if layer % 2 == 0:
            w = ev_w_in[i]
            conv_off = Q_RANK + KV_RANK + ROPE
            w_e = jnp.concatenate([w[:, conv_off:], w[:, :conv_off],
                                   jnp.zeros((d, IN_EVEN - w.shape[1]), F32)], axis=1).astype(BF16)
            wuq = jnp.pad(ev_w_uq[i].reshape(Q_RANK, HEADS, NOPE + ROPE),
                          ((0, 0), (0, 0), (0, HEAD_SLOT - NOPE - ROPE))).reshape(Q_RANK, HEADS * HEAD_SLOT).astype(BF16)
            wukv = ev_w_ukv[i].reshape(KV_RANK, HEADS, NOPE + VDIM)
            wuk = wukv[:, :, :NOPE].reshape(KV_RANK, HEADS * NOPE).astype(BF16)
            wuvt = wukv[:, :, NOPE:].reshape(KV_RANK, HEADS * VDIM).T.astype(BF16)
            pc, q, k, vt = _even_in(xs, modl, row2(norm1_g[layer]), w_e, row2(ev_q_g[i]), row2(ev_kv_g[i]),
                                    wuq, wuk, wuvt, rc, rsa, rsb, n_lat_tiles)
            conv_w = jnp.pad(ev_conv_w[i], ((0, 1), (0, 0)))
            c_out = _conv_branch(pc, conv_w, row2(ev_conv_b[i]), row2(ev_gn_g[i]), row2(ev_gn_b[i]), n_lat_tiles)
            a_lat = _attention(q, k, vt, tq=2 * TILE, q_tile0=0, n_q_tiles=n_lat_tiles // 2, kv_row0=0,
                               n_chunks=n_lat_tiles + 1)
            a_ctx = _attention(q, k, vt, tq=TILE, q_tile0=n_lat_tiles, n_q_tiles=1, kv_row0=n_lat, n_chunks=1)
            a_out = jnp.concatenate([a_lat, a_ctx], axis=1)
            parts = [a_out, c_out]
            w_out = ev_w_out[i].astype(BF16)
        else:
            gated = _odd_in(xs, modl, row2(norm1_g[layer]), od_w_in[i].astype(BF16), row2(od_ln_g[i]),
                            row2(od_ln_b[i]), od_w_s[i].astype(BF16), od_b_s[i].T, n_lat_tiles)
            parts = [gated]
            w_out = od_w_out[i].astype(BF16)
        x_mid, aff_t = _out_proj(parts, w_out, xs, modl, row2(norm2_g[layer]), moe_w_r[layer].T.astype(BF16),
                                 n_lat_tiles)
        sel, sx, tc = _topk(aff_t, segs, capt)
        idx = (sel[:, :, 0] * 256.0 + sel[:, :, 1]).astype(I32)
        pos = (sel[:, :, 2] * 256.0 + sel[:, :, 3]).astype(I32)
        gate = (sel[:, :, 4] + sel[:, :, 5]) + sel[:, :, 6]
        ip = jnp.concatenate([idx, pos], axis=-1).reshape(bsz * N_EXPERTS, 2 * capt)
        z = _expert_ffn(ip, gate.reshape(bsz * N_EXPERTS, capt, 1), x_mid, modl, row2(norm2_g[layer]),
                        moe_w1[layer].astype(BF16), moe_w3[layer].astype(BF16), moe_w2[layer].astype(BF16),
                        cap_l, capt)
        ts = jnp.concatenate([sx[:, 0, ::TILE], sx[:, 0, -1:] + tc[:, 0, -1:]], axis=1).astype(I32).reshape(-1)
        xs = _combine(ts, x_mid, sx, tc, modl, z, n_lat_tiles)
    return _final_norm(xs, row2(final_g), n_lat)
```

```python
import functools
import math

import jax
import jax.numpy as jnp
from jax import lax
from jax.experimental import pallas as pl
from jax.experimental.pallas import tpu as pltpu

F32 = jnp.float32
BF16 = jnp.bfloat16
I32 = jnp.int32
U32 = jnp.uint32

NORM_EPS = 1e-6
GRID_W = 64
ROPE_BASE = 10000.0
HEADS = 8
NOPE = 128
ROPE = 64
VDIM = 128
Q_RANK = 512
KV_RANK = 256
CONV_CH = 1024
CONV_GROUPS = 8
CONV_WIDTH = 31
GMLP_CH = 2048
GMLP_GROUPS = 8
CHUNK = 128
N_EXPERTS = 16
CAPACITY_FACTOR = 2

LANES = 128
TILE = 256
HALO = 16
HEAD_SLOT = 256
N_MOD = 6
CTX_MOD_ROW = 4
MOD_ROWS = 8
IN_EVEN = 3072
SEL_ROWS = 8
QK_COEF = (NOPE + ROPE) ** -0.5 * math.log2(math.e)
SUBLANES = 8
LAT_TQ = 1024


def _cparams(sem, vmem_mb):
    return pltpu.CompilerParams(dimension_semantics=sem, vmem_limit_bytes=vmem_mb << 20)


def _resident(shape):
    nd = len(shape)
    return pl.BlockSpec(shape, lambda *_: (0,) * nd, pipeline_mode=pl.Buffered(1))


def _rms(x, g):
    return x * lax.rsqrt(jnp.mean(x * x, axis=-1, keepdims=True) + NORM_EPS) * g


def _mod_spec(k, n_lat_tiles, d):
    return pl.BlockSpec((1, 1, d), lambda b, i: (jnp.where(i >= n_lat_tiles, CTX_MOD_ROW, b) * N_MOD + k, 0, 0))


def _mod_body(c_ref, w_ref, b_ref, o_ref):
    a = c_ref[...]
    a = a * jax.nn.sigmoid(a)
    o_ref[0] = jnp.dot(a.astype(BF16), w_ref[0].astype(BF16), preferred_element_type=F32) + b_ref[0]


def _modulation(cc, w_mod, b_mod):
    n_layers, d, n = w_mod.shape
    tn = max(t for t in (1024, 512, 256, LANES) if n % t == 0)
    return pl.pallas_call(
        _mod_body,
        grid=(n_layers, n // tn),
        in_specs=[pl.BlockSpec((MOD_ROWS, d), lambda l, j: (0, 0)),
                  pl.BlockSpec((1, d, tn), lambda l, j: (l, 0, j)),
                  pl.BlockSpec((1, 1, tn), lambda l, j: (l, 0, j))],
        out_specs=pl.BlockSpec((1, MOD_ROWS, tn), lambda l, j: (l, 0, j)),
        out_shape=jax.ShapeDtypeStruct((n_layers, MOD_ROWS, n), F32),
        compiler_params=_cparams(("arbitrary", "arbitrary"), 40),
        name="modulation",
    )(cc, w_mod, b_mod.reshape(n_layers, 1, n))


def _rope128(grp, c, sa, sb):
    return grp * c + pltpu.roll(grp, 96, 1) * sa + pltpu.roll(grp, 32, 1) * sb


def _even_in_body(x_ref, sh_ref, sc_ref, g_ref, w_ref, qg_ref, kvg_ref, wuq_ref, wuk_ref, wuvt_ref,
                  rc_ref, rsa_ref, rsb_ref, pc_ref, q_ref, k_ref, vt_ref):
    h = _rms(x_ref[0], g_ref[...]) * (1.0 + sc_ref[0]) + sh_ref[0]
    p = jnp.dot(h.astype(BF16), w_ref[...], preferred_element_type=F32)
    n_conv = 2 * CONV_CH
    pc_ref[0] = p[:, :n_conv].astype(BF16)
    pq = p[:, n_conv:n_conv + Q_RANK]
    pkv = p[:, n_conv + Q_RANK:n_conv + Q_RANK + KV_RANK]
    pkr = p[:, n_conv + Q_RANK + KV_RANK:n_conv + Q_RANK + KV_RANK + LANES]
    rc, rsa, rsb = rc_ref[...], rsa_ref[...], rsb_ref[...]
    q = jnp.dot(_rms(pq, qg_ref[...]).astype(BF16), wuq_ref[...], preferred_element_type=F32) * QK_COEF
    ckv = _rms(pkv, kvg_ref[...]).astype(BF16)
    kn = jnp.dot(ckv, wuk_ref[...], preferred_element_type=F32)
    kr = _rope128(pkr, rc, rsa, rsb).astype(BF16)
    vt = lax.dot_general(wuvt_ref[...], ckv, (((1,), (1,)), ((), ())), preferred_element_type=F32)
    for hd in range(HEADS):
        o = hd * HEAD_SLOT
        q_ref[0, :, o:o + NOPE] = q[:, o:o + NOPE].astype(BF16)
        q_ref[0, :, o + NOPE:o + HEAD_SLOT] = _rope128(q[:, o + NOPE:o + HEAD_SLOT], rc, rsa, rsb).astype(BF16)
        k_ref[0, :, o:o + NOPE] = kn[:, hd * NOPE:(hd + 1) * NOPE].astype(BF16)
        k_ref[0, :, o + NOPE:o + HEAD_SLOT] = kr
        vt_ref[0, hd, 0] = vt[hd * VDIM:(hd + 1) * VDIM, :].astype(BF16)


def _even_in(x, modl, g, w, qg, kvg, wuq, wuk, wuvt, rc, rsa, rsb, n_lat_tiles):
    b, s, d = x.shape
    nt = s // TILE
    tile = lambda n: pl.BlockSpec((1, TILE, n), lambda bb, i: (bb, i, 0))
    rope = pl.BlockSpec((TILE, LANES), lambda bb, i: (i, 0))
    return pl.pallas_call(
        _even_in_body,
        grid=(b, nt),
        in_specs=[tile(d), _mod_spec(0, n_lat_tiles, d), _mod_spec(1, n_lat_tiles, d), _resident((1, d)),
                  _resident(w.shape), _resident(qg.shape), _resident(kvg.shape), _resident(wuq.shape),
                  _resident(wuk.shape), _resident(wuvt.shape), rope, rope, rope],
        out_specs=[tile(2 * CONV_CH), tile(HEADS * HEAD_SLOT), tile(HEADS * HEAD_SLOT),
                   pl.BlockSpec((1, HEADS, 1, VDIM, TILE), lambda bb, i: (bb, 0, i, 0, 0))],
        out_shape=[jax.ShapeDtypeStruct((b, s, 2 * CONV_CH), BF16),
                   jax.ShapeDtypeStruct((b, s, HEADS * HEAD_SLOT), BF16),
                   jax.ShapeDtypeStruct((b, s, HEADS * HEAD_SLOT), BF16),
                   jax.ShapeDtypeStruct((b, HEADS, nt, VDIM, TILE), BF16)],
        compiler_params=_cparams(("arbitrary", "arbitrary"), 48),
        name="even_in",
    )(x, modl, modl, g, w, qg, kvg, wuq, wuk, wuvt, rc, rsa, rsb)


def _glu(v):
    return v[:, :CONV_CH].astype(F32) * jax.nn.sigmoid(v[:, CONV_CH:].astype(F32))


def _conv_body(cur_ref, prev_ref, next_ref, w_ref, b_ref, gg_ref, gb_ref, o_ref, ubuf, shift_s,
               *, n_lat_tiles, n_tiles):
    i = pl.program_id(1)
    left_ok = jnp.logical_and(i != 0, i != n_lat_tiles)
    right_ok = jnp.logical_and(i != n_lat_tiles - 1, i != n_tiles - 1)
    ubuf[0:HALO, :] = jnp.where(left_ok, _glu(prev_ref[0]), 0.0)
    ubuf[HALO:HALO + TILE, :] = _glu(cur_ref[0])
    ubuf[HALO + TILE:2 * HALO + TILE, :] = jnp.where(right_ok, _glu(next_ref[0]), 0.0)
    pad = CONV_WIDTH // 2
    gsz = CONV_CH // CONV_GROUPS
    rows = 64
    sh_rows = shift_s.shape[1]
    for grp in range(CONV_GROUPS):
        ls = slice(grp * gsz, (grp + 1) * gsz)
        for sft in range(1, SUBLANES):
            shift_s[sft - 1] = ubuf[sft:sft + sh_rows, ls]
        for r0 in range(0, TILE, rows):
            acc = jnp.zeros((rows, gsz), F32)
            for kk in range(CONV_WIDTH):
                blk, sft = divmod(HALO - pad + kk, SUBLANES)
                o = blk * SUBLANES + r0
                src = ubuf[o:o + rows, ls] if sft == 0 else shift_s[sft - 1, o:o + rows, :]
                acc = acc + w_ref[kk:kk + 1, ls] * src
            y = acc + b_ref[:, ls]
            mu = jnp.mean(y, axis=-1, keepdims=True)
            yc = y - mu
            var = jnp.mean(yc * yc, axis=-1, keepdims=True)
            yn = yc * lax.rsqrt(var + NORM_EPS) * gg_ref[:, ls] + gb_ref[:, ls]
            o_ref[0, r0:r0 + rows, ls] = (yn * jax.nn.sigmoid(yn)).astype(BF16)


def _conv_branch(pc, w, bias, gg, gb, n_lat_tiles):
    b, s, _ = pc.shape
    nt = s // TILE
    hb = TILE // HALO
    nh = s // HALO
    return pl.pallas_call(
        functools.partial(_conv_body, n_lat_tiles=n_lat_tiles, n_tiles=nt),
        grid=(b, nt),
        in_specs=[pl.BlockSpec((1, TILE, 2 * CONV_CH), lambda bb, i: (bb, i, 0)),
                  pl.BlockSpec((1, HALO, 2 * CONV_CH), lambda bb, i: (bb, jnp.maximum(i * hb - 1, 0), 0)),
                  pl.BlockSpec((1, HALO, 2 * CONV_CH), lambda bb, i: (bb, jnp.minimum((i + 1) * hb, nh - 1), 0)),
                  _resident(w.shape), _resident(bias.shape), _resident(gg.shape), _resident(gb.shape)],
        out_specs=pl.BlockSpec((1, TILE, CONV_CH), lambda bb, i: (bb, i, 0)),
        out_shape=jax.ShapeDtypeStruct((b, s, CONV_CH), BF16),
        scratch_shapes=[pltpu.VMEM((TILE + 2 * HALO, CONV_CH), F32),
                        pltpu.VMEM((SUBLANES - 1, TILE + 2 * HALO - SUBLANES, CONV_CH // CONV_GROUPS), F32)],
        compiler_params=_cparams(("arbitrary", "arbitrary"), 32),
        name="conv_branch",
    )(pc, pc, pc, w, bias, gg, gb)


def _attn_body(q_ref, k_ref, v_ref, o_ref, st_a, st_b, p_a, p_b, al_a, al_b, mx_a, mx_b, m_s, l_s, acc_s,
               *, n_chunks):
    q = q_ref[0]
    m_s[...] = jnp.full(m_s.shape, -jnp.inf, F32)
    l_s[...] = jnp.zeros(l_s.shape, F32)
    acc_s[...] = jnp.zeros(acc_s.shape, F32)
    bufs = ((st_a, p_a, al_a, mx_a), (st_b, p_b, al_b, mx_b))

    def scores(j, buf):
        kc = k_ref[0, pl.ds(pl.multiple_of(j * TILE, TILE), TILE), :]
        st = lax.dot_general(kc, q, (((1,), (1,)), ((), ())), preferred_element_type=F32)
        buf[0][...] = st
        buf[3][...] = jnp.max(st, axis=0, keepdims=True)

    def softmax(buf):
        st = buf[0][...]
        m_old = m_s[...]
        m_new = jnp.maximum(m_old, buf[3][...])
        alpha = jnp.exp2(m_old - m_new)
        p = jnp.exp2(st - m_new)
        l_s[...] = alpha * l_s[...] + jnp.sum(p, axis=0, keepdims=True)
        m_s[...] = m_new
        buf[1][...] = p.astype(BF16)
        buf[2][...] = alpha

    def weighted_values(j, buf):
        acc_s[...] = buf[2][...] * acc_s[...] + jnp.dot(v_ref[0, 0, j], buf[1][...], preferred_element_type=F32)

    def steady(j, cur, nxt):
        scores(j + 1, nxt)
        softmax(cur)
        weighted_values(j - 1, nxt)

    scores(0, bufs[0])
    if n_chunks > 1:
        scores(1, bufs[1])
    softmax(bufs[0])
    n_steady = max(n_chunks - 2, 0)

    def pair(i, carry):
        j = 1 + 2 * i
        steady(j, bufs[1], bufs[0])
        steady(j + 1, bufs[0], bufs[1])
        return carry

    lax.fori_loop(0, n_steady // 2, pair, 0)
    if n_steady % 2:
        j = n_chunks - 2
        steady(j, bufs[j % 2], bufs[(j + 1) % 2])
    if n_chunks > 1:
        softmax(bufs[(n_chunks - 1) % 2])
        weighted_values(n_chunks - 2, bufs[(n_chunks - 2) % 2])
    weighted_values(n_chunks - 1, bufs[(n_chunks - 1) % 2])
    o_ref[0] = (acc_s[...] / l_s[...]).T.astype(BF16)


def _attention(q, k, vt, *, tq, q_tile0, n_q_tiles, kv_row0, n_chunks):
    b, s, _ = q.shape
    kv_rows = n_chunks * TILE
    kb = kv_row0 // kv_rows
    cb = kv_row0 // TILE // n_chunks
    return pl.pallas_call(
        functools.partial(_attn_body, n_chunks=n_chunks),
        grid=(b, HEADS, n_q_tiles),
        in_specs=[pl.BlockSpec((1, tq, HEAD_SLOT), lambda bb, h, i: (bb, q_tile0 + i, h)),
                  pl.BlockSpec((1, kv_rows, HEAD_SLOT), lambda bb, h, i: (bb, kb, h)),
                  pl.BlockSpec((1, 1, n_chunks, VDIM, TILE), lambda bb, h, i: (bb, h, cb, 0, 0))],
        out_specs=pl.BlockSpec((1, tq, VDIM), lambda bb, h, i: (bb, i, h)),
        out_shape=jax.ShapeDtypeStruct((b, n_q_tiles * tq, HEADS * VDIM), BF16),
        scratch_shapes=[pltpu.VMEM((TILE, tq), F32), pltpu.VMEM((TILE, tq), F32),
                        pltpu.VMEM((TILE, tq), BF16), pltpu.VMEM((TILE, tq), BF16),
                        pltpu.VMEM((1, tq), F32), pltpu.VMEM((1, tq), F32),
                        pltpu.VMEM((1, tq), F32), pltpu.VMEM((1, tq), F32),
                        pltpu.VMEM((1, tq), F32), pltpu.VMEM((1, tq), F32), pltpu.VMEM((VDIM, tq), F32)],
        compiler_params=_cparams(("arbitrary", "arbitrary", "arbitrary"), 32),
        name="attention",
    )(q, k, vt)


def _gelu_tanh(x):
    return 0.5 * x * (1.0 + jnp.tanh(math.sqrt(2.0 / math.pi) * (x + 0.044715 * (x * x * x))))


def _odd_in_body(x_ref, sh_ref, sc_ref, g_ref, w_ref, lg_ref, lb_ref, ws_ref, bst_ref, o_ref):
    h = _rms(x_ref[0], g_ref[...]) * (1.0 + sc_ref[0]) + sh_ref[0]
    z = _gelu_tanh(jnp.dot(h.astype(BF16), w_ref[...], preferred_element_type=F32))
    u = z[:, :GMLP_CH]
    v = z[:, GMLP_CH:]
    mu = jnp.mean(v, axis=-1, keepdims=True)
    vc = v - mu
    var = jnp.mean(vc * vc, axis=-1, keepdims=True)
    vn = (vc * lax.rsqrt(var + NORM_EPS) * lg_ref[...] + lb_ref[...]).astype(BF16)
    gsz = GMLP_CH // GMLP_GROUPS
    for c0 in range(0, TILE, CHUNK):
        for grp in range(GMLP_GROUPS):
            ls = slice(grp * gsz, (grp + 1) * gsz)
            sg = jnp.dot(ws_ref[grp], vn[c0:c0 + CHUNK, ls], preferred_element_type=F32) + bst_ref[:, grp:grp + 1]
            o_ref[0, c0:c0 + CHUNK, ls] = (u[c0:c0 + CHUNK, ls] * sg).astype(BF16)


def _odd_in(x, modl, g, w, lg, lb, ws, bst, n_lat_tiles):
    b, s, d = x.shape
    nt = s // TILE
    return pl.pallas_call(
        _odd_in_body,
        grid=(b, nt),
        in_specs=[pl.BlockSpec((1, TILE, d), lambda bb, i: (bb, i, 0)),
                  _mod_spec(0, n_lat_tiles, d), _mod_spec(1, n_lat_tiles, d), _resident((1, d)),
                  _resident(w.shape), _resident(lg.shape), _resident(lb.shape), _resident(ws.shape),
                  _resident(bst.shape)],
        out_specs=pl.BlockSpec((1, TILE, GMLP_CH), lambda bb, i: (bb, i, 0)),
        out_shape=jax.ShapeDtypeStruct((b, s, GMLP_CH), BF16),
        compiler_params=_cparams(("arbitrary", "arbitrary"), 56),
        name="odd_in",
    )(x, modl, modl, g, w, lg, lb, ws, bst)


def _out_body(*refs, n_parts):
    a_refs = refs[:n_parts]
    w_ref, x_ref, g1_ref, n2_ref, sh_ref, sc_ref, wrt_ref, xo_ref, aff_ref = refs[n_parts:]
    y = None
    off = 0
    for a_ref in a_refs:
        kk = a_ref.shape[-1]
        t = jnp.dot(a_ref[0], w_ref[off:off + kk, :], preferred_element_type=F32)
        y = t if y is None else y + t
        off += kk
    xn = x_ref[0] + g1_ref[0] * y
    xo_ref[0] = xn
    h2 = _rms(xn, n2_ref[...]) * (1.0 + sc_ref[0]) + sh_ref[0]
    lg = lax.dot_general(wrt_ref[...], h2.astype(BF16), (((1,), (1,)), ((), ())), preferred_element_type=F32)
    e = jnp.exp(lg - jnp.max(lg, axis=0, keepdims=True))
    aff_ref[0] = e / jnp.sum(e, axis=0, keepdims=True)


def _out_proj(parts, w, x, modl, n2, wrt, n_lat_tiles):
    b, s, d = x.shape
    nt = s // TILE
    tile = lambda n: pl.BlockSpec((1, TILE, n), lambda bb, i: (bb, i, 0))
    return pl.pallas_call(
        functools.partial(_out_body, n_parts=len(parts)),
        grid=(b, nt),
        in_specs=[tile(a.shape[-1]) for a in parts]
        + [_resident(w.shape), tile(d), _mod_spec(2, n_lat_tiles, d), _resident((1, d)),
           _mod_spec(3, n_lat_tiles, d), _mod_spec(4, n_lat_tiles, d), _resident(wrt.shape)],
        out_specs=[tile(d), pl.BlockSpec((1, N_EXPERTS, TILE), lambda bb, i: (bb, 0, i))],
        out_shape=[jax.ShapeDtypeStruct((b, s, d), F32), jax.ShapeDtypeStruct((b, N_EXPERTS, s), F32)],
        compiler_params=_cparams(("arbitrary", "arbitrary"), 40),
        name="out_proj",
    )(*parts, w, x, modl, n2, modl, modl, wrt)


def _count(pred):
    return jnp.sum(pred.astype(F32), axis=1, keepdims=True)


def _cumsum_lanes(x):
    r, n = x.shape
    tri = (lax.broadcasted_iota(I32, (LANES, LANES), 0) <= lax.broadcasted_iota(I32, (LANES, LANES), 1)).astype(BF16)
    outs = []
    off = jnp.zeros((r, 1), F32)
    for blk in range(n // LANES):
        loc = jnp.dot(x[:, blk * LANES:(blk + 1) * LANES].astype(BF16), tri, preferred_element_type=F32) + off
        outs.append(loc)
        off = loc[:, LANES - 1:LANES]
    return jnp.concatenate(outs, axis=1)


def _split3(x):
    hi = x.astype(BF16).astype(F32)
    r1 = x - hi
    mid = r1.astype(BF16).astype(F32)
    lo = (r1 - mid).astype(BF16).astype(F32)
    return hi, mid, lo


def _topk_body(aff_ref, sel_ref, sx_ref, tc_ref, mask_s, pm_s, planes_s, res_s, *, segs):
    s_total = aff_ref.shape[-1]
    for off, n, cap, _ in segs:
        a = aff_ref[0, :, off:off + n]
        bits = pltpu.bitcast(a, I32)
        thr = jnp.zeros((N_EXPERTS, 1), I32)
        for bit in range(30, -1, -1):
            cand = thr | (1 << bit)
            thr = jnp.where(_count(bits >= cand) >= cap, cand, thr)
        gt = bits > thr
        eq = bits == thr
        need = cap - _count(gt)
        idx = lax.broadcasted_iota(I32, (N_EXPERTS, n), 1)
        j0 = jnp.zeros((N_EXPERTS, 1), I32)
        for bit in range(n.bit_length() - 1, -1, -1):
            cand = j0 | (1 << bit)
            j0 = jnp.where(_count(jnp.logical_and(eq, idx < cand)) < need, cand, j0)
        m = jnp.logical_or(gt, jnp.logical_and(eq, idx <= j0)).astype(F32)
        mask_s[:, off:off + n] = m
        pm_s[:, off:off + n] = _cumsum_lanes(m) * m

    m_all = mask_s[...]
    tc = jnp.sum(m_all, axis=0, keepdims=True)
    sx = _cumsum_lanes(jnp.broadcast_to(tc, (8, s_total)))[0:1] - tc
    tc_ref[0] = tc
    sx_ref[0] = sx
    lower = (lax.broadcasted_iota(I32, (N_EXPERTS, N_EXPERTS), 1)
             < lax.broadcasted_iota(I32, (N_EXPERTS, N_EXPERTS), 0)).astype(BF16)
    rank = jnp.dot(lower, m_all.astype(BF16), preferred_element_type=F32)
    pos = sx + rank
    tok = lax.broadcasted_iota(I32, (N_EXPERTS, s_total), 1).astype(F32)
    g_hi, g_mid, g_lo = _split3(aff_ref[0])
    tok_hi = jnp.floor(tok * (1.0 / 256.0))
    pos_hi = jnp.floor(pos * (1.0 / 256.0))
    planes = (tok_hi, tok - 256.0 * tok_hi, pos_hi, pos - 256.0 * pos_hi, g_hi, g_mid, g_lo, jnp.zeros_like(tok))
    for r, pln in enumerate(planes):
        planes_s[r * N_EXPERTS:(r + 1) * N_EXPERTS, :] = pln.astype(BF16)

    def per_expert(e, carry):
        for off, n, cap, slot0 in segs:
            blk = min(cap, TILE)
            prow = pm_s[pl.ds(e, 1), off:off + n]
            vals = planes_s[:, off:off + n]
            for c0 in range(0, cap, blk):
                slot = (lax.broadcasted_iota(I32, (blk, n), 0) + (c0 + 1)).astype(F32)
                onehot = (prow == slot).astype(BF16)
                res_s[:, 0:blk] = lax.dot_general(vals, onehot, (((1,), (1,)), ((), ())),
                                                  preferred_element_type=F32)
                for r in range(SEL_ROWS):
                    row = res_s[pl.ds(r * N_EXPERTS + e, 1), :]
                    sel_ref[0, e, r:r + 1, slot0 + c0:slot0 + c0 + blk] = row[:, 0:blk]
        return carry

    lax.fori_loop(0, N_EXPERTS, per_expert, 0)


def _topk(aff_t, segs, capt):
    b, e, s = aff_t.shape
    return pl.pallas_call(
        functools.partial(_topk_body, segs=segs),
        grid=(b,),
        in_specs=[pl.BlockSpec((1, e, s), lambda bb: (bb, 0, 0))],
        out_specs=[pl.BlockSpec((1, e, SEL_ROWS, capt), lambda bb: (bb, 0, 0, 0)),
                   pl.BlockSpec((1, 1, s), lambda bb: (bb, 0, 0)),
                   pl.BlockSpec((1, 1, s), lambda bb: (bb, 0, 0))],
        out_shape=[jax.ShapeDtypeStruct((b, e, SEL_ROWS, capt), F32),
                   jax.ShapeDtypeStruct((b, 1, s), F32), jax.ShapeDtypeStruct((b, 1, s), F32)],
        scratch_shapes=[pltpu.VMEM((e, s), F32), pltpu.VMEM((e, s), F32), pltpu.VMEM((SEL_ROWS * e, s), BF16),
                        pltpu.VMEM((SEL_ROWS * e, TILE), F32)],
        compiler_params=_cparams(("arbitrary",), 48),
        name="topk",
    )(aff_t)


def _pack_bf16_pairs(y):
    half = y.shape[1] // 2
    hi = pltpu.bitcast(y[:, :half].astype(BF16).astype(F32), U32)
    lo = pltpu.bitcast(y[:, half:].astype(BF16).astype(F32), U32)
    return hi | lax.shift_right_logical(lo, jnp.uint32(16))


def _unpack_bf16_pairs(w):
    hi = pltpu.bitcast(w & jnp.uint32(0xFFFF0000), F32).astype(BF16)
    lo = pltpu.bitcast(lax.shift_left(w, jnp.uint32(16)), F32).astype(BF16)
    return hi, lo


def _expert_body(ip_hbm, gate0_ref, gate1_ref, x_hbm, sh0_ref, sc0_ref, sh1_ref, sc1_ref, shc_ref, scc_ref, n2_ref,
                 w1_ref, w3_ref, w2_ref, z_hbm, ip0, ip1, ip_sem, xs_buf, gsem, y_buf, ssem, *, n_b, cap_l, capt):
    e = pl.program_id(0)
    pb = pl.program_id(1)
    n_e = pl.num_programs(0)
    n_pb = n_b // 2
    s0 = (e * n_pb + pb) * 2
    n_steps = n_e * n_b
    b0 = 2 * pb
    wrap = pb + 1 == n_pb
    e2 = jnp.where(wrap, e + 1, e)
    b2 = jnp.where(wrap, 0, b0 + 2)
    ips = (ip0, ip1)
    n_grp = capt // SUBLANES

    def ip_copy(ee, bb, u):
        return pltpu.make_async_copy(ip_hbm.at[bb * n_e + ee], ips[u], ip_sem.at[u])

    def gather_start(bb, u):
        def body(i, carry):
            for r in range(SUBLANES):
                row = ips[u][i * SUBLANES + r]
                pltpu.make_async_copy(
                    x_hbm.at[bb, lax.shift_right_logical(row, 3), pl.ds(row & (SUBLANES - 1), 1)],
                    xs_buf.at[u, i, pl.ds(r, 1)], gsem.at[u]).start()
            return carry
        lax.fori_loop(0, n_grp, body, 0)

    def scatter_start(bb, u):
        def body(i, carry):
            for r in range(SUBLANES):
                row = ips[u][capt + i * SUBLANES + r]
                pltpu.make_async_copy(
                    y_buf.at[i, pl.ds(r, 1)],
                    z_hbm.at[bb, lax.shift_right_logical(row, 3), pl.ds(row & (SUBLANES - 1), 1)], ssem).start()
            return carry
        lax.fori_loop(0, n_grp, body, 0)

    def scatter_wait():
        pltpu.make_async_copy(y_buf, z_hbm.at[0, pl.ds(0, n_grp)], ssem).wait()

    def compute(u, gate_ref, sh_ref, sc_ref):
        pltpu.make_async_copy(x_hbm.at[0, pl.ds(0, n_grp)], xs_buf.at[u], gsem.at[u]).wait()
        xs = xs_buf[u].reshape(capt, xs_buf.shape[-1])
        hn = _rms(xs, n2_ref[...])
        is_lat = lax.broadcasted_iota(I32, (capt, 1), 0) < cap_l
        sc = jnp.where(is_lat, sc_ref[0], scc_ref[0])
        sh = jnp.where(is_lat, sh_ref[0], shc_ref[0])
        hb = (hn * (1.0 + sc) + sh).astype(BF16)
        a1 = jnp.dot(hb, w1_ref[0, 0], preferred_element_type=F32)
        a3 = jnp.dot(hb, w3_ref[0, 0], preferred_element_type=F32)
        hid = (a1 * jax.nn.sigmoid(a1) * a3).astype(BF16)
        y = jnp.dot(hid, w2_ref[0, 0], preferred_element_type=F32) * gate_ref[0]
        return _pack_bf16_pairs(y).reshape(y_buf.shape)

    @pl.when(s0 == 0)
    def _():
        ip_copy(e, b0, 0).start()
        ip_copy(e, b0, 0).wait()
        gather_start(b0, 0)
        ip_copy(e, b0 + 1, 1).start()

    ip_copy(e, b0 + 1, 1).wait()
    gather_start(b0 + 1, 1)
    packed = compute(0, gate0_ref, sh0_ref, sc0_ref)

    @pl.when(s0 > 0)
    def _():
        scatter_wait()

    y_buf[...] = packed
    scatter_start(b0, 0)

    @pl.when(s0 + 2 < n_steps)
    def _():
        ip_copy(e2, b2, 0).start()

    @pl.when(s0 + 2 < n_steps)
    def _():
        ip_copy(e2, b2, 0).wait()
        gather_start(b2, 0)

    packed = compute(1, gate1_ref, sh1_ref, sc1_ref)
    scatter_wait()
    y_buf[...] = packed
    scatter_start(b0 + 1, 1)

    @pl.when(s0 + 3 < n_steps)
    def _():
        ip_copy(e2, b2 + 1, 1).start()

    @pl.when(s0 + 2 == n_steps)
    def _():
        scatter_wait()


def _expert_ffn(ip, gate, x4, modl, n2, w1, w3, w2, layer, cap_l, capt):
    bsz, _, _, d = x4.shape
    _, n_e, _, f = w1.shape
    assert bsz % 2 == 0 and capt % SUBLANES == 0
    ne = n_e * capt
    any_spec = pl.BlockSpec(memory_space=pl.ANY)
    gate_spec = lambda u: pl.BlockSpec((1, capt, 1), lambda e, pb: ((2 * pb + u) * n_e + e, 0, 0))
    mod_b = lambda u, k: pl.BlockSpec((1, 1, d), lambda e, pb: ((2 * pb + u) * N_MOD + k, 0, 0))
    mod_c = lambda k: pl.BlockSpec((1, 1, d), lambda e, pb: (CTX_MOD_ROW * N_MOD + k, 0, 0))
    return pl.pallas_call(
        functools.partial(_expert_body, n_b=bsz, cap_l=cap_l, capt=capt),
        grid=(n_e, bsz // 2),
        in_specs=[any_spec, gate_spec(0), gate_spec(1), any_spec,
                  mod_b(0, 3), mod_b(0, 4), mod_b(1, 3), mod_b(1, 4), mod_c(3), mod_c(4), _resident((1, d)),
                  pl.BlockSpec((1, 1, d, f), lambda e, pb: (layer, e, 0, 0)),
                  pl.BlockSpec((1, 1, d, f), lambda e, pb: (layer, e, 0, 0)),
                  pl.BlockSpec((1, 1, f, d), lambda e, pb: (layer, e, 0, 0))],
        out_specs=any_spec,
        out_shape=jax.ShapeDtypeStruct((bsz, ne // SUBLANES, SUBLANES, d // 2), U32),
        scratch_shapes=[pltpu.SMEM((2 * capt,), I32), pltpu.SMEM((2 * capt,), I32), pltpu.SemaphoreType.DMA((2,)),
                        pltpu.VMEM((2, capt // SUBLANES, SUBLANES, d), F32), pltpu.SemaphoreType.DMA((2,)),
                        pltpu.VMEM((capt // SUBLANES, SUBLANES, d // 2), U32), pltpu.SemaphoreType.DMA(())],
        compiler_params=_cparams(("arbitrary", "arbitrary"), 60),
        name="expert_ffn",
    )(ip, gate, gate, x4, modl, modl, modl, modl, modl, modl, n2, w1, w3, w2)


def _combine_body(ts_ref, x_ref, sx_ref, tc_ref, g2_ref, z_hbm, o_ref, zbuf, zsem, acc_s, *, n_tiles, ne):
    b = pl.program_id(0)
    i = pl.program_id(1)
    n_b = pl.num_programs(0)
    half = acc_s.shape[1] // 2
    grp = TILE // SUBLANES

    def tile_range(bb, ii):
        e_lo = ts_ref[bb * (n_tiles + 1) + ii]
        e_hi = ts_ref[bb * (n_tiles + 1) + ii + 1]
        base = lax.shift_left(lax.shift_right_logical(e_lo, 3), 3)
        return base, lax.shift_right_logical(e_hi - base + (TILE - 1), TILE.bit_length() - 1)

    def chunk_start(base, k):
        return jnp.minimum(base + k * TILE, ne - TILE)

    def chunk_copy(bb, base, k, sl):
        g0 = lax.shift_right_logical(chunk_start(base, k), 3)
        return pltpu.make_async_copy(z_hbm.at[bb, pl.ds(g0, grp)], zbuf.at[sl], zsem.at[sl])

    base, n_ch = tile_range(b, i)
    acc_s[...] = jnp.zeros(acc_s.shape, F32)
    sx = sx_ref[0]
    end = sx + tc_ref[0]

    @pl.when(jnp.logical_and(jnp.logical_and(b == 0, i == 0), n_ch > 0))
    def _():
        chunk_copy(b, base, 0, 0).start()

    def body(k, carry):
        sl = lax.rem(k, 2)

        @pl.when(k + 1 < n_ch)
        def _():
            chunk_copy(b, base, k + 1, 1 - sl).start()

        chunk_copy(b, base, k, sl).wait()
        ids = (chunk_start(base, k) + lax.broadcasted_iota(I32, (TILE, 1), 0))
        idf = ids.astype(F32)
        own = jnp.logical_and(jnp.logical_and(ids >= base + k * TILE, idf >= sx), idf < end)
        own = own.astype(BF16)
        hi, lo = _unpack_bf16_pairs(zbuf[sl].reshape(TILE, half))
        dims = (((0,), (0,)), ((), ()))
        acc_s[:, :half] += lax.dot_general(own, hi, dims, preferred_element_type=F32)
        acc_s[:, half:] += lax.dot_general(own, lo, dims, preferred_element_type=F32)
        return carry

    lax.fori_loop(0, n_ch, body, 0)

    last_tile = i + 1 == n_tiles
    nb = jnp.where(last_tile, b + 1, b)
    ni = jnp.where(last_tile, 0, i + 1)

    @pl.when(nb < n_b)
    def _():
        nbase, nn_ch = tile_range(nb, ni)

        @pl.when(nn_ch > 0)
        def _():
            chunk_copy(nb, nbase, 0, 0).start()

    o_ref[0] = x_ref[0] + g2_ref[0] * acc_s[...]


def _combine(ts, x, sx, tc, modl, z, n_lat_tiles):
    bsz, s, d = x.shape
    nt = s // TILE
    ne = z.shape[1] * SUBLANES
    grid_spec = pltpu.PrefetchScalarGridSpec(
        num_scalar_prefetch=1,
        grid=(bsz, nt),
        in_specs=[pl.BlockSpec((1, TILE, d), lambda b, i, t: (b, i, 0)),
                  pl.BlockSpec((1, 1, TILE), lambda b, i, t: (b, 0, i)),
                  pl.BlockSpec((1, 1, TILE), lambda b, i, t: (b, 0, i)),
                  pl.BlockSpec((1, 1, d), lambda b, i, t: (jnp.where(i >= n_lat_tiles, CTX_MOD_ROW, b) * N_MOD + 5, 0, 0)),
                  pl.BlockSpec(memory_space=pl.ANY)],
        out_specs=pl.BlockSpec((1, TILE, d), lambda b, i, t: (b, i, 0)),
        scratch_shapes=[pltpu.VMEM((2, TILE // SUBLANES, SUBLANES, d // 2), U32), pltpu.SemaphoreType.DMA((2,)),
                        pltpu.VMEM((TILE, d), F32)],
    )
    return pl.pallas_call(
        functools.partial(_combine_body, n_tiles=nt, ne=ne),
        grid_spec=grid_spec,
        out_shape=jax.ShapeDtypeStruct((bsz, s, d), F32),
        compiler_params=_cparams(("arbitrary", "arbitrary"), 32),
        name="combine",
    )(ts, x, sx, tc, modl, z)


def _final_body(x_ref, g_ref, o_ref):
    o_ref[0] = _rms(x_ref[0], g_ref[...])


def _final_norm(x, g, n_lat):
    bsz, _, d = x.shape
    return pl.pallas_call(
        _final_body,
        grid=(bsz, n_lat // TILE),
        in_specs=[pl.BlockSpec((1, TILE, d), lambda b, i: (b, i, 0)), _resident((1, d))],
        out_specs=pl.BlockSpec((1, TILE, d), lambda b, i: (b, i, 0)),
        out_shape=jax.ShapeDtypeStruct((bsz, n_lat, d), F32),
        compiler_params=_cparams(("arbitrary", "arbitrary"), 32),
        name="final_norm",
    )(x, g)


def _rope_tables(n_lat, n_ctx):
    rows = n_lat // GRID_W
    row, col = jnp.meshgrid(jnp.arange(rows, dtype=F32), jnp.arange(GRID_W, dtype=F32), indexing="ij")
    row, col = row.reshape(-1), col.reshape(-1)
    n_freq = ROPE // 4
    inv_freq = ROPE_BASE ** (-jnp.arange(n_freq, dtype=F32) / n_freq)
    ang = jnp.concatenate([row[:, None] * inv_freq, col[:, None] * inv_freq], axis=-1)
    cos, sin = jnp.cos(ang), jnp.sin(ang)
    half = ROPE // 2
    zero = lambda n: jnp.zeros((n_lat, n), F32)
    rc = jnp.concatenate([cos, cos, jnp.ones((n_lat, LANES - ROPE), F32)], axis=1)
    rsa = jnp.concatenate([-sin, zero(LANES - half)], axis=1)
    rsb = jnp.concatenate([zero(half), sin, zero(LANES - ROPE)], axis=1)
    ident = jnp.ones((n_ctx, LANES), F32)
    none = jnp.zeros((n_ctx, LANES), F32)
    return (jnp.concatenate([rc, ident]), jnp.concatenate([rsa, none]), jnp.concatenate([rsb, none]))


def kernel(x, c, ctx, c_ctx, w_mod, b_mod, norm1_g, norm2_g, ev_w_in, ev_q_g, ev_kv_g, ev_w_uq, ev_w_ukv, ev_conv_w, ev_conv_b, ev_gn_g, ev_gn_b, ev_w_out, od_w_in, od_ln_g, od_ln_b, od_w_s, od_b_s, od_w_out, moe_w_r, moe_w1, moe_w3, moe_w2, final_g):
    bsz, n_lat, d = x.shape
    n_ctx = ctx.shape[1]
    depth = w_mod.shape[0]
    assert bsz <= CTX_MOD_ROW and n_lat % LAT_TQ == 0 and n_ctx == TILE and d % LANES == 0
    n_lat_tiles = n_lat // TILE
    cap_l = CAPACITY_FACTOR * n_lat // N_EXPERTS
    cap_c = CAPACITY_FACTOR * n_ctx // N_EXPERTS
    capt = cap_l + cap_c
    segs = ((0, n_lat, cap_l, 0), (n_lat, n_ctx, cap_c, cap_l))

    cc = jnp.zeros((MOD_ROWS, d), F32).at[:bsz].set(c).at[CTX_MOD_ROW].set(c_ctx)
    mod = _modulation(cc, w_mod, b_mod)
    xs = jnp.concatenate([x, ctx], axis=1)
    rc, rsa, rsb = _rope_tables(n_lat, n_ctx)
    row2 = lambda v: v.reshape(1, -1)
    w1_all, w3_all, w2_all = moe_w1.astype(BF16), moe_w3.astype(BF16), moe_w2.astype(BF16)

    for layer in range(depth):
        i = layer // 2
        modl = mod[layer].reshape(MOD_ROWS * N_MOD, 1, d)
        if layer % 2 == 0:
            w = ev_w_in[i]
            conv_off = Q_RANK + KV_RANK + ROPE
            w_e = jnp.concatenate([w[:, conv_off:], w[:, :conv_off],
                                   jnp.zeros((d, IN_EVEN - w.shape[1]), F32)], axis=1).astype(BF16)
            wuq = jnp.pad(ev_w_uq[i].reshape(Q_RANK, HEADS, NOPE + ROPE),
                          ((0, 0), (0, 0), (0, HEAD_SLOT - NOPE - ROPE))).reshape(Q_RANK, HEADS * HEAD_SLOT).astype(BF16)
            wukv = ev_w_ukv[i].reshape(KV_RANK, HEADS, NOPE + VDIM)
            wuk = wukv[:, :, :NOPE].reshape(KV_RANK, HEADS * NOPE).astype(BF16)
            wuvt = wukv[:, :, NOPE:].reshape(KV_RANK, HEADS * VDIM).T.astype(BF16)
            pc, q, k, vt = _even_in(xs, modl, row2(norm1_g[layer]), w_e, row2(ev_q_g[i]), row2(ev_kv_g[i]),
                                    wuq, wuk, wuvt, rc, rsa, rsb, n_lat_tiles)
            conv_w = jnp.pad(ev_conv_w[i], ((0, 1), (0, 0)))
            c_out = _conv_branch(pc, conv_w, row2(ev_conv_b[i]), row2(ev_gn_g[i]), row2(ev_gn_b[i]), n_lat_tiles)
            a_lat = _attention(q, k, vt, tq=LAT_TQ, q_tile0=0, n_q_tiles=n_lat // LAT_TQ, kv_row0=0,
                               n_chunks=n_lat_tiles + 1)
            a_ctx = _attention(q, k, vt, tq=TILE, q_tile0=n_lat_tiles, n_q_tiles=1, kv_row0=n_lat, n_chunks=1)
            a_out = jnp.concatenate([a_lat, a_ctx], axis=1)
            parts = [a_out, c_out]
            w_out = ev_w_out[i].astype(BF16)
        else:
            gated = _odd_in(xs, modl, row2(norm1_g[layer]), od_w_in[i].astype(BF16), row2(od_ln_g[i]),
                            row2(od_ln_b[i]), od_w_s[i].astype(BF16), od_b_s[i].T, n_lat_tiles)
            parts = [gated]
            w_out = od_w_out[i].astype(BF16)
        x_mid, aff_t = _out_proj(parts, w_out, xs, modl, row2(norm2_g[layer]), moe_w_r[layer].T.astype(BF16),
                                 n_lat_tiles)
        sel, sx, tc = _topk(aff_t, segs, capt)
        idx = (sel[:, :, 0] * 256.0 + sel[:, :, 1]).astype(I32)
        pos = (sel[:, :, 2] * 256.0 + sel[:, :, 3]).astype(I32)
        gate = (sel[:, :, 4] + sel[:, :, 5]) + sel[:, :, 6]
        ip = jnp.concatenate([idx, pos], axis=-1).reshape(bsz * N_EXPERTS, 2 * capt)
        z = _expert_ffn(ip, gate.reshape(bsz * N_EXPERTS, capt, 1),
                        x_mid.reshape(bsz, (n_lat + n_ctx) // SUBLANES, SUBLANES, d), modl, row2(norm2_g[layer]),
                        w1_all, w3_all, w2_all, layer, cap_l, capt)
        ts = jnp.concatenate([sx[:, 0, ::TILE], sx[:, 0, -1:] + tc[:, 0, -1:]], axis=1).astype(I32).reshape(-1)
        xs = _combine(ts, x_mid, sx, tc, modl, z, n_lat_tiles)
    return _final_norm(xs, row2(final_g), n_lat)
```

```python
import functools
import math

import jax
import jax.numpy as jnp
from jax import lax
from jax.experimental import pallas as pl
from jax.experimental.pallas import tpu as pltpu

F32 = jnp.float32
BF16 = jnp.bfloat16
I32 = jnp.int32
U32 = jnp.uint32

NORM_EPS = 1e-6
GRID_W = 64
ROPE_BASE = 10000.0
HEADS = 8
NOPE = 128
ROPE = 64
VDIM = 128
Q_RANK = 512
KV_RANK = 256
CONV_CH = 1024
CONV_GROUPS = 8
CONV_WIDTH = 31
GMLP_CH = 2048
GMLP_GROUPS = 8
CHUNK = 128
N_EXPERTS = 16
CAPACITY_FACTOR = 2

LANES = 128
TILE = 256
HALO = 16
HEAD_SLOT = 256
N_MOD = 6
CTX_MOD_ROW = 4
MOD_ROWS = 8
IN_EVEN = 3072
SEL_ROWS = 8
QK_COEF = (NOPE + ROPE) ** -0.5 * math.log2(math.e)
SUBLANES = 8
LAT_TQ = 4096
COMBINE_AHEAD = 2


def _cparams(sem, vmem_mb, flags=None):
    return pltpu.CompilerParams(dimension_semantics=sem, vmem_limit_bytes=vmem_mb << 20, flags=flags)


def _resident(shape):
    nd = len(shape)
    return pl.BlockSpec(shape, lambda *_: (0,) * nd, pipeline_mode=pl.Buffered(1))


def _rms(x, g):
    return x * lax.rsqrt(jnp.mean(x * x, axis=-1, keepdims=True) + NORM_EPS) * g


def _mod_spec(k, n_lat_tiles, d):
    return pl.BlockSpec((1, 1, d), lambda b, i: (jnp.where(i >= n_lat_tiles, CTX_MOD_ROW, b) * N_MOD + k, 0, 0))


def _mod_body(c_ref, w_ref, b_ref, o_ref):
    a = c_ref[...]
    a = a * jax.nn.sigmoid(a)
    o_ref[0] = jnp.dot(a.astype(BF16), w_ref[0].astype(BF16), preferred_element_type=F32) + b_ref[0]


def _modulation(cc, w_mod, b_mod):
    n_layers, d, n = w_mod.shape
    tn = max(t for t in (1024, 512, 256, LANES) if n % t == 0)
    return pl.pallas_call(
        _mod_body,
        grid=(n_layers, n // tn),
        in_specs=[pl.BlockSpec((MOD_ROWS, d), lambda l, j: (0, 0)),
                  pl.BlockSpec((1, d, tn), lambda l, j: (l, 0, j)),
                  pl.BlockSpec((1, 1, tn), lambda l, j: (l, 0, j))],
        out_specs=pl.BlockSpec((1, MOD_ROWS, tn), lambda l, j: (l, 0, j)),
        out_shape=jax.ShapeDtypeStruct((n_layers, MOD_ROWS, n), F32),
        compiler_params=_cparams(("arbitrary", "arbitrary"), 40),
        name="modulation",
    )(cc, w_mod, b_mod.reshape(n_layers, 1, n))


def _rope128(grp, c, sa, sb):
    return grp * c + pltpu.roll(grp, 96, 1) * sa + pltpu.roll(grp, 32, 1) * sb


def _even_in_body(x_ref, sh_ref, sc_ref, g_ref, w_ref, qg_ref, kvg_ref, wuq_ref, wuk_ref, wuvt_ref,
                  rc_ref, rsa_ref, rsb_ref, pc_ref, q_ref, k_ref, vt_ref):
    h = _rms(x_ref[0], g_ref[...]) * (1.0 + sc_ref[0]) + sh_ref[0]
    p = jnp.dot(h.astype(BF16), w_ref[...], preferred_element_type=F32)
    n_conv = 2 * CONV_CH
    pc_ref[0] = p[:, :n_conv].astype(BF16)
    pq = p[:, n_conv:n_conv + Q_RANK]
    pkv = p[:, n_conv + Q_RANK:n_conv + Q_RANK + KV_RANK]
    pkr = p[:, n_conv + Q_RANK + KV_RANK:n_conv + Q_RANK + KV_RANK + LANES]
    rc, rsa, rsb = rc_ref[...], rsa_ref[...], rsb_ref[...]
    q = jnp.dot(_rms(pq, qg_ref[...]).astype(BF16), wuq_ref[...], preferred_element_type=F32) * QK_COEF
    ckv = _rms(pkv, kvg_ref[...]).astype(BF16)
    kn = jnp.dot(ckv, wuk_ref[...], preferred_element_type=F32)
    kr = _rope128(pkr, rc, rsa, rsb).astype(BF16)
    vt = lax.dot_general(wuvt_ref[...], ckv, (((1,), (1,)), ((), ())), preferred_element_type=F32)
    for hd in range(HEADS):
        o = hd * HEAD_SLOT
        q_ref[0, :, o:o + NOPE] = q[:, o:o + NOPE].astype(BF16)
        q_ref[0, :, o + NOPE:o + HEAD_SLOT] = _rope128(q[:, o + NOPE:o + HEAD_SLOT], rc, rsa, rsb).astype(BF16)
        k_ref[0, :, o:o + NOPE] = kn[:, hd * NOPE:(hd + 1) * NOPE].astype(BF16)
        k_ref[0, :, o + NOPE:o + HEAD_SLOT] = kr
        vt_ref[0, hd, 0] = vt[hd * VDIM:(hd + 1) * VDIM, :].astype(BF16)


def _even_in(x, modl, g, w, qg, kvg, wuq, wuk, wuvt, rc, rsa, rsb, n_lat_tiles):
    b, s, d = x.shape
    nt = s // TILE
    tile = lambda n: pl.BlockSpec((1, TILE, n), lambda bb, i: (bb, i, 0))
    rope = pl.BlockSpec((TILE, LANES), lambda bb, i: (i, 0))
    return pl.pallas_call(
        _even_in_body,
        grid=(b, nt),
        in_specs=[tile(d), _mod_spec(0, n_lat_tiles, d), _mod_spec(1, n_lat_tiles, d), _resident((1, d)),
                  _resident(w.shape), _resident(qg.shape), _resident(kvg.shape), _resident(wuq.shape),
                  _resident(wuk.shape), _resident(wuvt.shape), rope, rope, rope],
        out_specs=[tile(2 * CONV_CH), tile(HEADS * HEAD_SLOT), tile(HEADS * HEAD_SLOT),
                   pl.BlockSpec((1, HEADS, 1, VDIM, TILE), lambda bb, i: (bb, 0, i, 0, 0))],
        out_shape=[jax.ShapeDtypeStruct((b, s, 2 * CONV_CH), BF16),
                   jax.ShapeDtypeStruct((b, s, HEADS * HEAD_SLOT), BF16),
                   jax.ShapeDtypeStruct((b, s, HEADS * HEAD_SLOT), BF16),
                   jax.ShapeDtypeStruct((b, HEADS, nt, VDIM, TILE), BF16)],
        compiler_params=_cparams(("arbitrary", "arbitrary"), 48),
        name="even_in",
    )(x, modl, modl, g, w, qg, kvg, wuq, wuk, wuvt, rc, rsa, rsb)


def _glu(v):
    return v[:, :CONV_CH].astype(F32) * jax.nn.sigmoid(v[:, CONV_CH:].astype(F32))


def _conv_body(cur_ref, prev_ref, next_ref, w_ref, b_ref, gg_ref, gb_ref, o_ref, ubuf, shift_s,
               *, n_lat_tiles, n_tiles):
    i = pl.program_id(1)
    left_ok = jnp.logical_and(i != 0, i != n_lat_tiles)
    right_ok = jnp.logical_and(i != n_lat_tiles - 1, i != n_tiles - 1)
    ubuf[0:HALO, :] = jnp.where(left_ok, _glu(prev_ref[0]), 0.0)
    ubuf[HALO:HALO + TILE, :] = _glu(cur_ref[0])
    ubuf[HALO + TILE:2 * HALO + TILE, :] = jnp.where(right_ok, _glu(next_ref[0]), 0.0)
    pad = CONV_WIDTH // 2
    gsz = CONV_CH // CONV_GROUPS
    rows = 64
    sh_rows = shift_s.shape[1]
    for grp in range(CONV_GROUPS):
        ls = slice(grp * gsz, (grp + 1) * gsz)
        for sft in range(1, SUBLANES):
            shift_s[sft - 1] = ubuf[sft:sft + sh_rows, ls]
        for r0 in range(0, TILE, rows):
            acc = jnp.zeros((rows, gsz), F32)
            for kk in range(CONV_WIDTH):
                blk, sft = divmod(HALO - pad + kk, SUBLANES)
                o = blk * SUBLANES + r0
                src = ubuf[o:o + rows, ls] if sft == 0 else shift_s[sft - 1, o:o + rows, :]
                acc = acc + w_ref[kk:kk + 1, ls] * src
            y = acc + b_ref[:, ls]
            mu = jnp.mean(y, axis=-1, keepdims=True)
            yc = y - mu
            var = jnp.mean(yc * yc, axis=-1, keepdims=True)
            yn = yc * lax.rsqrt(var + NORM_EPS) * gg_ref[:, ls] + gb_ref[:, ls]
            o_ref[0, r0:r0 + rows, ls] = (yn * jax.nn.sigmoid(yn)).astype(BF16)


def _conv_branch(pc, w, bias, gg, gb, n_lat_tiles):
    b, s, _ = pc.shape
    nt = s // TILE
    hb = TILE // HALO
    nh = s // HALO
    return pl.pallas_call(
        functools.partial(_conv_body, n_lat_tiles=n_lat_tiles, n_tiles=nt),
        grid=(b, nt),
        in_specs=[pl.BlockSpec((1, TILE, 2 * CONV_CH), lambda bb, i: (bb, i, 0)),
                  pl.BlockSpec((1, HALO, 2 * CONV_CH), lambda bb, i: (bb, jnp.maximum(i * hb - 1, 0), 0)),
                  pl.BlockSpec((1, HALO, 2 * CONV_CH), lambda bb, i: (bb, jnp.minimum((i + 1) * hb, nh - 1), 0)),
                  _resident(w.shape), _resident(bias.shape), _resident(gg.shape), _resident(gb.shape)],
        out_specs=pl.BlockSpec((1, TILE, CONV_CH), lambda bb, i: (bb, i, 0)),
        out_shape=jax.ShapeDtypeStruct((b, s, CONV_CH), BF16),
        scratch_shapes=[pltpu.VMEM((TILE + 2 * HALO, CONV_CH), F32),
                        pltpu.VMEM((SUBLANES - 1, TILE + 2 * HALO - SUBLANES, CONV_CH // CONV_GROUPS), F32)],
        compiler_params=_cparams(("arbitrary", "arbitrary"), 32),
        name="conv_branch",
    )(pc, pc, pc, w, bias, gg, gb)


ATTN_RING = 4


def _attn_body(q_ref, k_ref, v_ref, o_ref, *scratch, n_chunks):
    bufs = tuple(scratch[4 * r:4 * r + 4] for r in range(ATTN_RING))
    m_s, l_s, acc_s = scratch[4 * ATTN_RING:]
    q = q_ref[0]
    m_s[...] = jnp.full(m_s.shape, -jnp.inf, F32)
    l_s[...] = jnp.zeros(l_s.shape, F32)
    acc_s[...] = jnp.zeros(acc_s.shape, F32)
    ring = len(bufs)
    lag = ring // 2

    def scores(j, buf):
        kc = k_ref[0, pl.ds(pl.multiple_of(j * TILE, TILE), TILE), :]
        st = lax.dot_general(kc, q, (((1,), (1,)), ((), ())), preferred_element_type=F32)
        buf[0][...] = st
        buf[3][...] = jnp.max(st, axis=0, keepdims=True)

    def softmax(buf):
        st = buf[0][...]
        m_old = m_s[...]
        m_new = jnp.maximum(m_old, buf[3][...])
        alpha = jnp.exp2(m_old - m_new)
        p = jnp.exp2(st - m_new)
        l_s[...] = alpha * l_s[...] + jnp.sum(p, axis=0, keepdims=True)
        m_s[...] = m_new
        buf[1][...] = p.astype(BF16)
        buf[2][...] = alpha

    def weighted_values(j, buf):
        acc_s[...] = buf[2][...] * acc_s[...] + jnp.dot(v_ref[0, 0, j], buf[1][...], preferred_element_type=F32)

    def stage(t, t_static):
        if 0 <= t_static + lag < n_chunks or not isinstance(t, int):
            scores(t + lag, bufs[(t_static + lag) % ring])
        if 0 <= t_static < n_chunks or not isinstance(t, int):
            softmax(bufs[t_static % ring])
        if 0 <= t_static - lag < n_chunks or not isinstance(t, int):
            weighted_values(t - lag, bufs[(t_static - lag) % ring])

    t_full0 = lag
    n_full = max(n_chunks - 2 * lag, 0)
    n_body = n_full // ring
    for t in range(-lag, min(t_full0, n_chunks + lag)):
        stage(t, t)

    def body(i, carry):
        t0 = t_full0 + ring * i
        for r in range(ring):
            stage(t0 + r, t_full0 + r)
        return carry

    lax.fori_loop(0, n_body, body, 0)
    for t in range(t_full0 + n_body * ring, n_chunks + lag):
        if t >= t_full0:
            stage(t, t)
    o_ref[0] = (acc_s[...] / l_s[...]).T.astype(BF16)


def _attention(q, k, vt, *, tq, q_tile0, n_q_tiles, kv_row0, n_chunks):
    b, s, _ = q.shape
    kv_rows = n_chunks * TILE
    kb = kv_row0 // kv_rows
    cb = kv_row0 // TILE // n_chunks
    return pl.pallas_call(
        functools.partial(_attn_body, n_chunks=n_chunks),
        grid=(b, HEADS, n_q_tiles),
        in_specs=[pl.BlockSpec((1, tq, HEAD_SLOT), lambda bb, h, i: (bb, q_tile0 + i, h)),
                  pl.BlockSpec((1, kv_rows, HEAD_SLOT), lambda bb, h, i: (bb, kb, h)),
                  pl.BlockSpec((1, 1, n_chunks, VDIM, TILE), lambda bb, h, i: (bb, h, cb, 0, 0))],
        out_specs=pl.BlockSpec((1, tq, VDIM), lambda bb, h, i: (bb, i, h)),
        out_shape=jax.ShapeDtypeStruct((b, n_q_tiles * tq, HEADS * VDIM), BF16),
        scratch_shapes=[pltpu.VMEM((TILE, tq), F32), pltpu.VMEM((TILE, tq), BF16),
                        pltpu.VMEM((1, tq), F32), pltpu.VMEM((1, tq), F32)] * ATTN_RING
        + [pltpu.VMEM((1, tq), F32), pltpu.VMEM((1, tq), F32), pltpu.VMEM((VDIM, tq), F32)],
        compiler_params=_cparams(("arbitrary", "arbitrary", "arbitrary"), 58),
        name="attention",
    )(q, k, vt)


def _gelu_tanh(x):
    return 0.5 * x * (1.0 + jnp.tanh(math.sqrt(2.0 / math.pi) * (x + 0.044715 * (x * x * x))))


def _odd_in_body(x_ref, sh_ref, sc_ref, g_ref, w_ref, lg_ref, lb_ref, ws_ref, bst_ref, o_ref):
    h = _rms(x_ref[0], g_ref[...]) * (1.0 + sc_ref[0]) + sh_ref[0]
    z = _gelu_tanh(jnp.dot(h.astype(BF16), w_ref[...], preferred_element_type=F32))
    u = z[:, :GMLP_CH]
    v = z[:, GMLP_CH:]
    mu = jnp.mean(v, axis=-1, keepdims=True)
    vc = v - mu
    var = jnp.mean(vc * vc, axis=-1, keepdims=True)
    vn = (vc * lax.rsqrt(var + NORM_EPS) * lg_ref[...] + lb_ref[...]).astype(BF16)
    gsz = GMLP_CH // GMLP_GROUPS
    for c0 in range(0, TILE, CHUNK):
        for grp in range(GMLP_GROUPS):
            ls = slice(grp * gsz, (grp + 1) * gsz)
            sg = jnp.dot(ws_ref[grp], vn[c0:c0 + CHUNK, ls], preferred_element_type=F32) + bst_ref[:, grp:grp + 1]
            o_ref[0, c0:c0 + CHUNK, ls] = (u[c0:c0 + CHUNK, ls] * sg).astype(BF16)


def _odd_in(x, modl, g, w, lg, lb, ws, bst, n_lat_tiles):
    b, s, d = x.shape
    nt = s // TILE
    return pl.pallas_call(
        _odd_in_body,
        grid=(b, nt),
        in_specs=[pl.BlockSpec((1, TILE, d), lambda bb, i: (bb, i, 0)),
                  _mod_spec(0, n_lat_tiles, d), _mod_spec(1, n_lat_tiles, d), _resident((1, d)),
                  _resident(w.shape), _resident(lg.shape), _resident(lb.shape), _resident(ws.shape),
                  _resident(bst.shape)],
        out_specs=pl.BlockSpec((1, TILE, GMLP_CH), lambda bb, i: (bb, i, 0)),
        out_shape=jax.ShapeDtypeStruct((b, s, GMLP_CH), BF16),
        compiler_params=_cparams(("arbitrary", "arbitrary"), 56),
        name="odd_in",
    )(x, modl, modl, g, w, lg, lb, ws, bst)


def _out_body(*refs, n_parts):
    a_refs = refs[:n_parts]
    w_ref, x_ref, g1_ref, n2_ref, sh_ref, sc_ref, wrt_ref, xo_ref, aff_ref = refs[n_parts:]
    y = None
    off = 0
    for a_ref in a_refs:
        kk = a_ref.shape[-1]
        t = jnp.dot(a_ref[0], w_ref[off:off + kk, :], preferred_element_type=F32)
        y = t if y is None else y + t
        off += kk
    xn = x_ref[0] + g1_ref[0] * y
    xo_ref[0] = xn
    h2 = _rms(xn, n2_ref[...]) * (1.0 + sc_ref[0]) + sh_ref[0]
    lg = lax.dot_general(wrt_ref[...], h2.astype(BF16), (((1,), (1,)), ((), ())), preferred_element_type=F32)
    e = jnp.exp(lg - jnp.max(lg, axis=0, keepdims=True))
    aff_ref[0] = e / jnp.sum(e, axis=0, keepdims=True)


def _out_proj(parts, w, x, modl, n2, wrt, n_lat_tiles):
    b, s, d = x.shape
    nt = s // TILE
    tile = lambda n: pl.BlockSpec((1, TILE, n), lambda bb, i: (bb, i, 0))
    return pl.pallas_call(
        functools.partial(_out_body, n_parts=len(parts)),
        grid=(b, nt),
        in_specs=[tile(a.shape[-1]) for a in parts]
        + [_resident(w.shape), tile(d), _mod_spec(2, n_lat_tiles, d), _resident((1, d)),
           _mod_spec(3, n_lat_tiles, d), _mod_spec(4, n_lat_tiles, d), _resident(wrt.shape)],
        out_specs=[tile(d), pl.BlockSpec((1, N_EXPERTS, TILE), lambda bb, i: (bb, 0, i))],
        out_shape=[jax.ShapeDtypeStruct((b, s, d), F32), jax.ShapeDtypeStruct((b, N_EXPERTS, s), F32)],
        compiler_params=_cparams(("arbitrary", "arbitrary"), 40),
        name="out_proj",
    )(*parts, w, x, modl, n2, modl, modl, wrt)


def _count(pred):
    return jnp.sum(pred.astype(F32), axis=1, keepdims=True)


def _cumsum_lanes(x):
    r, n = x.shape
    tri = (lax.broadcasted_iota(I32, (LANES, LANES), 0) <= lax.broadcasted_iota(I32, (LANES, LANES), 1)).astype(BF16)
    outs = []
    off = jnp.zeros((r, 1), F32)
    for blk in range(n // LANES):
        loc = jnp.dot(x[:, blk * LANES:(blk + 1) * LANES].astype(BF16), tri, preferred_element_type=F32) + off
        outs.append(loc)
        off = loc[:, LANES - 1:LANES]
    return jnp.concatenate(outs, axis=1)


def _split3(x):
    hi = x.astype(BF16).astype(F32)
    r1 = x - hi
    mid = r1.astype(BF16).astype(F32)
    lo = (r1 - mid).astype(BF16).astype(F32)
    return hi, mid, lo


def _topk_body(aff_ref, sel_ref, sx_ref, tc_ref, mask_s, pm_s, planes_s, res_s, *, segs):
    s_total = aff_ref.shape[-1]
    for off, n, cap, _ in segs:
        a = aff_ref[0, :, off:off + n]
        bits = pltpu.bitcast(a, I32)
        thr = jnp.zeros((N_EXPERTS, 1), I32)
        for bit in range(30, -1, -1):
            cand = thr | (1 << bit)
            thr = jnp.where(_count(bits >= cand) >= cap, cand, thr)
        gt = bits > thr
        eq = bits == thr
        need = cap - _count(gt)
        idx = lax.broadcasted_iota(I32, (N_EXPERTS, n), 1)
        j0 = jnp.zeros((N_EXPERTS, 1), I32)
        for bit in range(n.bit_length() - 1, -1, -1):
            cand = j0 | (1 << bit)
            j0 = jnp.where(_count(jnp.logical_and(eq, idx < cand)) < need, cand, j0)
        m = jnp.logical_or(gt, jnp.logical_and(eq, idx <= j0)).astype(F32)
        mask_s[:, off:off + n] = m
        pm_s[:, off:off + n] = _cumsum_lanes(m) * m

    m_all = mask_s[...]
    tc = jnp.sum(m_all, axis=0, keepdims=True)
    sx = _cumsum_lanes(jnp.broadcast_to(tc, (8, s_total)))[0:1] - tc
    tc_ref[0] = tc
    sx_ref[0] = sx
    lower = (lax.broadcasted_iota(I32, (N_EXPERTS, N_EXPERTS), 1)
             < lax.broadcasted_iota(I32, (N_EXPERTS, N_EXPERTS), 0)).astype(BF16)
    rank = jnp.dot(lower, m_all.astype(BF16), preferred_element_type=F32)
    pos = sx + rank
    tok = lax.broadcasted_iota(I32, (N_EXPERTS, s_total), 1).astype(F32)
    g_hi, g_mid, g_lo = _split3(aff_ref[0])
    tok_hi = jnp.floor(tok * (1.0 / 256.0))
    pos_hi = jnp.floor(pos * (1.0 / 256.0))
    planes = (tok_hi, tok - 256.0 * tok_hi, pos_hi, pos - 256.0 * pos_hi, g_hi, g_mid, g_lo, jnp.zeros_like(tok))
    for r, pln in enumerate(planes):
        planes_s[r * N_EXPERTS:(r + 1) * N_EXPERTS, :] = pln.astype(BF16)

    def per_expert(e, carry):
        for off, n, cap, slot0 in segs:
            blk = min(cap, TILE)
            prow = pm_s[pl.ds(e, 1), off:off + n]
            vals = planes_s[:, off:off + n]
            for c0 in range(0, cap, blk):
                slot = (lax.broadcasted_iota(I32, (blk, n), 0) + (c0 + 1)).astype(F32)
                onehot = (prow == slot).astype(BF16)
                res_s[:, 0:blk] = lax.dot_general(vals, onehot, (((1,), (1,)), ((), ())),
                                                  preferred_element_type=F32)
                for r in range(SEL_ROWS):
                    row = res_s[pl.ds(r * N_EXPERTS + e, 1), :]
                    sel_ref[0, e, r:r + 1, slot0 + c0:slot0 + c0 + blk] = row[:, 0:blk]
        return carry

    lax.fori_loop(0, N_EXPERTS, per_expert, 0)


def _topk(aff_t, segs, capt):
    b, e, s = aff_t.shape
    return pl.pallas_call(
        functools.partial(_topk_body, segs=segs),
        grid=(b,),
        in_specs=[pl.BlockSpec((1, e, s), lambda bb: (bb, 0, 0))],
        out_specs=[pl.BlockSpec((1, e, SEL_ROWS, capt), lambda bb: (bb, 0, 0, 0)),
                   pl.BlockSpec((1, 1, s), lambda bb: (bb, 0, 0)),
                   pl.BlockSpec((1, 1, s), lambda bb: (bb, 0, 0))],
        out_shape=[jax.ShapeDtypeStruct((b, e, SEL_ROWS, capt), F32),
                   jax.ShapeDtypeStruct((b, 1, s), F32), jax.ShapeDtypeStruct((b, 1, s), F32)],
        scratch_shapes=[pltpu.VMEM((e, s), F32), pltpu.VMEM((e, s), F32), pltpu.VMEM((SEL_ROWS * e, s), BF16),
                        pltpu.VMEM((SEL_ROWS * e, TILE), F32)],
        compiler_params=_cparams(("arbitrary",), 48),
        name="topk",
    )(aff_t)


def _pack_bf16_pairs(y):
    half = y.shape[1] // 2
    hi = pltpu.bitcast(y[:, :half].astype(BF16).astype(F32), U32)
    lo = pltpu.bitcast(y[:, half:].astype(BF16).astype(F32), U32)
    return hi | lax.shift_right_logical(lo, jnp.uint32(16))


def _unpack_bf16_pairs(w):
    hi = pltpu.bitcast(w & jnp.uint32(0xFFFF0000), F32).astype(BF16)
    lo = pltpu.bitcast(lax.shift_left(w, jnp.uint32(16)), F32).astype(BF16)
    return hi, lo


def _expert_body(ip_hbm, gate0_ref, gate1_ref, x_hbm, sh0_ref, sc0_ref, sh1_ref, sc1_ref, shc_ref, scc_ref, n2_ref,
                 w1_ref, w3_ref, w2_ref, z_hbm, ip0, ip1, ip_sem, xs_buf, gsem, y_buf, ssem, *, n_b, cap_l, capt):
    e = pl.program_id(0)
    pb = pl.program_id(1)
    n_e = pl.num_programs(0)
    n_pb = n_b // 2
    s0 = (e * n_pb + pb) * 2
    n_steps = n_e * n_b
    b0 = 2 * pb
    wrap = pb + 1 == n_pb
    e2 = jnp.where(wrap, e + 1, e)
    b2 = jnp.where(wrap, 0, b0 + 2)
    ips = (ip0, ip1)
    n_grp = capt // SUBLANES

    def ip_copy(ee, bb, u):
        return pltpu.make_async_copy(ip_hbm.at[bb * n_e + ee], ips[u], ip_sem.at[u])

    def gather_start(bb, u):
        def body(i, carry):
            for r in range(SUBLANES):
                row = ips[u][i * SUBLANES + r]
                pltpu.make_async_copy(
                    x_hbm.at[bb, lax.shift_right_logical(row, 3), pl.ds(row & (SUBLANES - 1), 1)],
                    xs_buf.at[u, i, pl.ds(r, 1)], gsem.at[u]).start()
            return carry
        lax.fori_loop(0, n_grp, body, 0)

    def scatter_start(bb, u):
        def body(i, carry):
            for r in range(SUBLANES):
                row = ips[u][capt + i * SUBLANES + r]
                pltpu.make_async_copy(
                    y_buf.at[i, pl.ds(r, 1)],
                    z_hbm.at[bb, lax.shift_right_logical(row, 3), pl.ds(row & (SUBLANES - 1), 1)], ssem).start()
            return carry
        lax.fori_loop(0, n_grp, body, 0)

    def scatter_wait():
        pltpu.make_async_copy(y_buf, z_hbm.at[0, pl.ds(0, n_grp)], ssem).wait()

    def compute(u, gate_ref, sh_ref, sc_ref):
        pltpu.make_async_copy(x_hbm.at[0, pl.ds(0, n_grp)], xs_buf.at[u], gsem.at[u]).wait()
        xs = xs_buf[u].reshape(capt, xs_buf.shape[-1])
        hn = _rms(xs, n2_ref[...])
        is_lat = lax.broadcasted_iota(I32, (capt, 1), 0) < cap_l
        sc = jnp.where(is_lat, sc_ref[0], scc_ref[0])
        sh = jnp.where(is_lat, sh_ref[0], shc_ref[0])
        hb = (hn * (1.0 + sc) + sh).astype(BF16)
        a1 = jnp.dot(hb, w1_ref[0, 0], preferred_element_type=F32)
        a3 = jnp.dot(hb, w3_ref[0, 0], preferred_element_type=F32)
        hid = (a1 * jax.nn.sigmoid(a1) * a3).astype(BF16)
        y = jnp.dot(hid, w2_ref[0, 0], preferred_element_type=F32) * gate_ref[0]
        return _pack_bf16_pairs(y).reshape(y_buf.shape)

    @pl.when(s0 == 0)
    def _():
        ip_copy(e, b0, 0).start()
        ip_copy(e, b0, 0).wait()
        gather_start(b0, 0)
        ip_copy(e, b0 + 1, 1).start()

    ip_copy(e, b0 + 1, 1).wait()
    gather_start(b0 + 1, 1)
    packed = compute(0, gate0_ref, sh0_ref, sc0_ref)

    @pl.when(s0 > 0)
    def _():
        scatter_wait()

    y_buf[...] = packed
    scatter_start(b0, 0)

    @pl.when(s0 + 2 < n_steps)
    def _():
        ip_copy(e2, b2, 0).start()

    @pl.when(s0 + 2 < n_steps)
    def _():
        ip_copy(e2, b2, 0).wait()
        gather_start(b2, 0)

    packed = compute(1, gate1_ref, sh1_ref, sc1_ref)
    scatter_wait()
    y_buf[...] = packed
    scatter_start(b0 + 1, 1)

    @pl.when(s0 + 3 < n_steps)
    def _():
        ip_copy(e2, b2 + 1, 1).start()

    @pl.when(s0 + 2 == n_steps)
    def _():
        scatter_wait()


def _expert_ffn(ip, gate, x4, modl, n2, w1, w3, w2, layer, cap_l, capt):
    bsz, _, _, d = x4.shape
    _, n_e, _, f = w1.shape
    assert bsz % 2 == 0 and capt % SUBLANES == 0
    ne = n_e * capt
    any_spec = pl.BlockSpec(memory_space=pl.ANY)
    gate_spec = lambda u: pl.BlockSpec((1, capt, 1), lambda e, pb: ((2 * pb + u) * n_e + e, 0, 0))
    mod_b = lambda u, k: pl.BlockSpec((1, 1, d), lambda e, pb: ((2 * pb + u) * N_MOD + k, 0, 0))
    mod_c = lambda k: pl.BlockSpec((1, 1, d), lambda e, pb: (CTX_MOD_ROW * N_MOD + k, 0, 0))
    return pl.pallas_call(
        functools.partial(_expert_body, n_b=bsz, cap_l=cap_l, capt=capt),
        grid=(n_e, bsz // 2),
        in_specs=[any_spec, gate_spec(0), gate_spec(1), any_spec,
                  mod_b(0, 3), mod_b(0, 4), mod_b(1, 3), mod_b(1, 4), mod_c(3), mod_c(4), _resident((1, d)),
                  pl.BlockSpec((1, 1, d, f), lambda e, pb: (layer, e, 0, 0)),
                  pl.BlockSpec((1, 1, d, f), lambda e, pb: (layer, e, 0, 0)),
                  pl.BlockSpec((1, 1, f, d), lambda e, pb: (layer, e, 0, 0))],
        out_specs=any_spec,
        out_shape=jax.ShapeDtypeStruct((bsz, ne // SUBLANES, SUBLANES, d // 2), U32),
        scratch_shapes=[pltpu.SMEM((2 * capt,), I32), pltpu.SMEM((2 * capt,), I32), pltpu.SemaphoreType.DMA((2,)),
                        pltpu.VMEM((2, capt // SUBLANES, SUBLANES, d), F32), pltpu.SemaphoreType.DMA((2,)),
                        pltpu.VMEM((capt // SUBLANES, SUBLANES, d // 2), U32), pltpu.SemaphoreType.DMA(())],
        compiler_params=_cparams(("arbitrary", "arbitrary"), 60),
        name="expert_ffn",
    )(ip, gate, gate, x4, modl, modl, modl, modl, modl, modl, n2, w1, w3, w2)


def _combine_body(ts_ref, x_ref, sx_ref, tc_ref, g2_ref, z_hbm, o_ref, zbuf, zsem, acc_s, *, n_tiles, ne):
    b = pl.program_id(0)
    i = pl.program_id(1)
    n_b = pl.num_programs(0)
    half = acc_s.shape[1] // 2
    grp = TILE // SUBLANES

    def tile_range(bb, ii):
        e_lo = ts_ref[bb * (n_tiles + 1) + ii]
        e_hi = ts_ref[bb * (n_tiles + 1) + ii + 1]
        base = lax.shift_left(lax.shift_right_logical(e_lo, 3), 3)
        return base, lax.shift_right_logical(e_hi - base + (TILE - 1), TILE.bit_length() - 1)

    def chunk_start(base, k):
        return jnp.minimum(base + k * TILE, ne - TILE)

    def chunk_copy(bb, base, k, sl):
        g0 = lax.shift_right_logical(chunk_start(base, k), 3)
        return pltpu.make_async_copy(z_hbm.at[bb, pl.ds(g0, grp)], zbuf.at[sl], zsem.at[sl])

    base, n_ch = tile_range(b, i)
    acc_s[...] = jnp.zeros(acc_s.shape, F32)
    sx = sx_ref[0]
    end = sx + tc_ref[0]

    def start_head(bb, hbase, hn_ch):
        for k in range(COMBINE_AHEAD):
            @pl.when(hn_ch > k)
            def _():
                chunk_copy(bb, hbase, k, k).start()

    @pl.when(jnp.logical_and(b == 0, i == 0))
    def _():
        start_head(b, base, n_ch)

    def body(k, carry):
        sl = lax.rem(k, COMBINE_AHEAD + 1)

        @pl.when(k + COMBINE_AHEAD < n_ch)
        def _():
            chunk_copy(b, base, k + COMBINE_AHEAD, lax.rem(k + COMBINE_AHEAD, COMBINE_AHEAD + 1)).start()

        chunk_copy(b, base, k, sl).wait()
        ids = (chunk_start(base, k) + lax.broadcasted_iota(I32, (TILE, 1), 0))
        idf = ids.astype(F32)
        own = jnp.logical_and(jnp.logical_and(ids >= base + k * TILE, idf >= sx), idf < end)
        own = own.astype(BF16)
        hi, lo = _unpack_bf16_pairs(zbuf[sl].reshape(TILE, half))
        dims = (((0,), (0,)), ((), ()))
        acc_s[:, :half] += lax.dot_general(own, hi, dims, preferred_element_type=F32)
        acc_s[:, half:] += lax.dot_general(own, lo, dims, preferred_element_type=F32)
        return carry

    lax.fori_loop(0, n_ch, body, 0)

    last_tile = i + 1 == n_tiles
    nb = jnp.where(last_tile, b + 1, b)
    ni = jnp.where(last_tile, 0, i + 1)

    @pl.when(nb < n_b)
    def _():
        nbase, nn_ch = tile_range(nb, ni)
        start_head(nb, nbase, nn_ch)

    o_ref[0] = x_ref[0] + g2_ref[0] * acc_s[...]


def _combine(ts, x, sx, tc, modl, z, n_lat_tiles):
    bsz, s, d = x.shape
    nt = s // TILE
    ne = z.shape[1] * SUBLANES
    grid_spec = pltpu.PrefetchScalarGridSpec(
        num_scalar_prefetch=1,
        grid=(bsz, nt),
        in_specs=[pl.BlockSpec((1, TILE, d), lambda b, i, t: (b, i, 0)),
                  pl.BlockSpec((1, 1, TILE), lambda b, i, t: (b, 0, i)),
                  pl.BlockSpec((1, 1, TILE), lambda b, i, t: (b, 0, i)),
                  pl.BlockSpec((1, 1, d), lambda b, i, t: (jnp.where(i >= n_lat_tiles, CTX_MOD_ROW, b) * N_MOD + 5, 0, 0)),
                  pl.BlockSpec(memory_space=pl.ANY)],
        out_specs=pl.BlockSpec((1, TILE, d), lambda b, i, t: (b, i, 0)),
        scratch_shapes=[pltpu.VMEM((COMBINE_AHEAD + 1, TILE // SUBLANES, SUBLANES, d // 2), U32),
                        pltpu.SemaphoreType.DMA((COMBINE_AHEAD + 1,)),
                        pltpu.VMEM((TILE, d), F32)],
    )
    return pl.pallas_call(
        functools.partial(_combine_body, n_tiles=nt, ne=ne),
        grid_spec=grid_spec,
        out_shape=jax.ShapeDtypeStruct((bsz, s, d), F32),
        compiler_params=_cparams(("arbitrary", "arbitrary"), 32),
        name="combine",
    )(ts, x, sx, tc, modl, z)


def _final_body(x_ref, g_ref, o_ref):
    o_ref[0] = _rms(x_ref[0], g_ref[...])


def _final_norm(x, g, n_lat):
    bsz, _, d = x.shape
    return pl.pallas_call(
        _final_body,
        grid=(bsz, n_lat // TILE),
        in_specs=[pl.BlockSpec((1, TILE, d), lambda b, i: (b, i, 0)), _resident((1, d))],
        out_specs=pl.BlockSpec((1, TILE, d), lambda b, i: (b, i, 0)),
        out_shape=jax.ShapeDtypeStruct((bsz, n_lat, d), F32),
        compiler_params=_cparams(("arbitrary", "arbitrary"), 32),
        name="final_norm",
    )(x, g)


def _rope_tables(n_lat, n_ctx):
    rows = n_lat // GRID_W
    row, col = jnp.meshgrid(jnp.arange(rows, dtype=F32), jnp.arange(GRID_W, dtype=F32), indexing="ij")
    row, col = row.reshape(-1), col.reshape(-1)
    n_freq = ROPE // 4
    inv_freq = ROPE_BASE ** (-jnp.arange(n_freq, dtype=F32) / n_freq)
    ang = jnp.concatenate([row[:, None] * inv_freq, col[:, None] * inv_freq], axis=-1)
    cos, sin = jnp.cos(ang), jnp.sin(ang)
    half = ROPE // 2
    zero = lambda n: jnp.zeros((n_lat, n), F32)
    rc = jnp.concatenate([cos, cos, jnp.ones((n_lat, LANES - ROPE), F32)], axis=1)
    rsa = jnp.concatenate([-sin, zero(LANES - half)], axis=1)
    rsb = jnp.concatenate([zero(half), sin, zero(LANES - ROPE)], axis=1)
    ident = jnp.ones((n_ctx, LANES), F32)
    none = jnp.zeros((n_ctx, LANES), F32)
    return (jnp.concatenate([rc, ident]), jnp.concatenate([rsa, none]), jnp.concatenate([rsb, none]))


def kernel(x, c, ctx, c_ctx, w_mod, b_mod, norm1_g, norm2_g, ev_w_in, ev_q_g, ev_kv_g, ev_w_uq, ev_w_ukv, ev_conv_w, ev_conv_b, ev_gn_g, ev_gn_b, ev_w_out, od_w_in, od_ln_g, od_ln_b, od_w_s, od_b_s, od_w_out, moe_w_r, moe_w1, moe_w3, moe_w2, final_g):
    bsz, n_lat, d = x.shape
    n_ctx = ctx.shape[1]
    depth = w_mod.shape[0]
    lat_tq = min(LAT_TQ, n_lat)
    assert bsz <= CTX_MOD_ROW and n_lat % lat_tq == 0 and n_ctx == TILE and d % LANES == 0
    n_lat_tiles = n_lat // TILE
    cap_l = CAPACITY_FACTOR * n_lat // N_EXPERTS
    cap_c = CAPACITY_FACTOR * n_ctx // N_EXPERTS
    capt = cap_l + cap_c
    segs = ((0, n_lat, cap_l, 0), (n_lat, n_ctx, cap_c, cap_l))

    cc = jnp.zeros((MOD_ROWS, d), F32).at[:bsz].set(c).at[CTX_MOD_ROW].set(c_ctx)
    mod = _modulation(cc, w_mod, b_mod)
    xs = jnp.concatenate([x, ctx], axis=1)
    rc, rsa, rsb = _rope_tables(n_lat, n_ctx)
    row2 = lambda v: v.reshape(1, -1)
    w1_all, w3_all, w2_all = moe_w1.astype(BF16), moe_w3.astype(BF16), moe_w2.astype(BF16)

    for layer in range(depth):
        i = layer // 2
        modl = mod[layer].reshape(MOD_ROWS * N_MOD, 1, d)
        if layer % 2 == 0:
            w = ev_w_in[i]
            conv_off = Q_RANK + KV_RANK + ROPE
            w_e = jnp.concatenate([w[:, conv_off:], w[:, :conv_off],
                                   jnp.zeros((d, IN_EVEN - w.shape[1]), F32)], axis=1).astype(BF16)
            wuq = jnp.pad(ev_w_uq[i].reshape(Q_RANK, HEADS, NOPE + ROPE),
                          ((0, 0), (0, 0), (0, HEAD_SLOT - NOPE - ROPE))).reshape(Q_RANK, HEADS * HEAD_SLOT).astype(BF16)
            wukv = ev_w_ukv[i].reshape(KV_RANK, HEADS, NOPE + VDIM)
            wuk = wukv[:, :, :NOPE].reshape(KV_RANK, HEADS * NOPE).astype(BF16)
            wuvt = wukv[:, :, NOPE:].reshape(KV_RANK, HEADS * VDIM).T.astype(BF16)
            pc, q, k, vt = _even_in(xs, modl, row2(norm1_g[layer]), w_e, row2(ev_q_g[i]), row2(ev_kv_g[i]),
                                    wuq, wuk, wuvt, rc, rsa, rsb, n_lat_tiles)
            conv_w = jnp.pad(ev_conv_w[i], ((0, 1), (0, 0)))
            c_out = _conv_branch(pc, conv_w, row2(ev_conv_b[i]), row2(ev_gn_g[i]), row2(ev_gn_b[i]), n_lat_tiles)
            a_lat = _attention(q, k, vt, tq=lat_tq, q_tile0=0, n_q_tiles=n_lat // lat_tq, kv_row0=0,
                               n_chunks=n_lat_tiles + 1)
            a_ctx = _attention(q, k, vt, tq=TILE, q_tile0=n_lat_tiles, n_q_tiles=1, kv_row0=n_lat, n_chunks=1)
            a_out = jnp.concatenate([a_lat, a_ctx], axis=1)
            parts = [a_out, c_out]
            w_out = ev_w_out[i].astype(BF16)
        else:
            gated = _odd_in(xs, modl, row2(norm1_g[layer]), od_w_in[i].astype(BF16), row2(od_ln_g[i]),
                            row2(od_ln_b[i]), od_w_s[i].astype(BF16), od_b_s[i].T, n_lat_tiles)
            parts = [gated]
            w_out = od_w_out[i].astype(BF16)
        x_mid, aff_t = _out_proj(parts, w_out, xs, modl, row2(norm2_g[layer]), moe_w_r[layer].T.astype(BF16),
                                 n_lat_tiles)
        sel, sx, tc = _topk(aff_t, segs, capt)
        idx = (sel[:, :, 0] * 256.0 + sel[:, :, 1]).astype(I32)
        pos = (sel[:, :, 2] * 256.0 + sel[:, :, 3]).astype(I32)
        gate = (sel[:, :, 4] + sel[:, :, 5]) + sel[:, :, 6]
        ip = jnp.concatenate([idx, pos], axis=-1).reshape(bsz * N_EXPERTS, 2 * capt)
        z = _expert_ffn(ip, gate.reshape(bsz * N_EXPERTS, capt, 1),
                        x_mid.reshape(bsz, (n_lat + n_ctx) // SUBLANES, SUBLANES, d), modl, row2(norm2_g[layer]),
                        w1_all, w3_all, w2_all, layer, cap_l, capt)
        ts = jnp.concatenate([sx[:, 0, ::TILE], sx[:, 0, -1:] + tc[:, 0, -1:]], axis=1).astype(I32).reshape(-1)
        xs = _combine(ts, x_mid, sx, tc, modl, z, n_lat_tiles)
    return _final_norm(xs, row2(final_g), n_lat)
```

```python
import functools
import math

import jax
import jax.numpy as jnp
from jax import lax
from jax.experimental import pallas as pl
from jax.experimental.pallas import tpu as pltpu

F32 = jnp.float32
BF16 = jnp.bfloat16
I32 = jnp.int32
U32 = jnp.uint32

NORM_EPS = 1e-6
GRID_W = 64
ROPE_BASE = 10000.0
HEADS = 8
NOPE = 128
ROPE = 64
VDIM = 128
Q_RANK = 512
KV_RANK = 256
CONV_CH = 1024
CONV_GROUPS = 8
CONV_WIDTH = 31
GMLP_CH = 2048
GMLP_GROUPS = 8
CHUNK = 128
N_EXPERTS = 16
CAPACITY_FACTOR = 2

LANES = 128
TILE = 256
HALO = 16
HEAD_SLOT = 256
N_MOD = 6
CTX_MOD_ROW = 4
MOD_ROWS = 8
IN_EVEN = 3072
SEL_ROWS = 8
QK_COEF = (NOPE + ROPE) ** -0.5 * math.log2(math.e)
SUBLANES = 8
LAT_TQ = 4096
COMBINE_AHEAD = 2


def _cparams(sem, vmem_mb, flags=None):
    return pltpu.CompilerParams(dimension_semantics=sem, vmem_limit_bytes=vmem_mb << 20, flags=flags)


def _resident(shape):
    nd = len(shape)
    return pl.BlockSpec(shape, lambda *_: (0,) * nd, pipeline_mode=pl.Buffered(1))


def _rms(x, g):
    return x * lax.rsqrt(jnp.mean(x * x, axis=-1, keepdims=True) + NORM_EPS) * g


def _mod_spec(k, n_lat_tiles, d):
    return pl.BlockSpec((1, 1, d), lambda b, i: (jnp.where(i >= n_lat_tiles, CTX_MOD_ROW, b) * N_MOD + k, 0, 0))


def _mod_body(c_ref, w_ref, b_ref, o_ref):
    a = c_ref[...]
    a = a * jax.nn.sigmoid(a)
    o_ref[0] = jnp.dot(a.astype(BF16), w_ref[0].astype(BF16), preferred_element_type=F32) + b_ref[0]


def _modulation(cc, w_mod, b_mod):
    n_layers, d, n = w_mod.shape
    tn = max(t for t in (1024, 512, 256, LANES) if n % t == 0)
    return pl.pallas_call(
        _mod_body,
        grid=(n_layers, n // tn),
        in_specs=[pl.BlockSpec((MOD_ROWS, d), lambda l, j: (0, 0)),
                  pl.BlockSpec((1, d, tn), lambda l, j: (l, 0, j)),
                  pl.BlockSpec((1, 1, tn), lambda l, j: (l, 0, j))],
        out_specs=pl.BlockSpec((1, MOD_ROWS, tn), lambda l, j: (l, 0, j)),
        out_shape=jax.ShapeDtypeStruct((n_layers, MOD_ROWS, n), F32),
        compiler_params=_cparams(("arbitrary", "arbitrary"), 40),
        name="modulation",
    )(cc, w_mod, b_mod.reshape(n_layers, 1, n))


def _rope128(grp, c, sa, sb):
    return grp * c + pltpu.roll(grp, 96, 1) * sa + pltpu.roll(grp, 32, 1) * sb


def _even_in_body(x_ref, sh_ref, sc_ref, g_ref, w_ref, qg_ref, kvg_ref, wuq_ref, wuk_ref, wuvt_ref,
                  rc_ref, rsa_ref, rsb_ref, pc_ref, q_ref, k_ref, vt_ref):
    h = _rms(x_ref[0], g_ref[...]) * (1.0 + sc_ref[0]) + sh_ref[0]
    p = jnp.dot(h.astype(BF16), w_ref[...], preferred_element_type=F32)
    n_conv = 2 * CONV_CH
    pc_ref[0] = p[:, :n_conv].astype(BF16)
    pq = p[:, n_conv:n_conv + Q_RANK]
    pkv = p[:, n_conv + Q_RANK:n_conv + Q_RANK + KV_RANK]
    pkr = p[:, n_conv + Q_RANK + KV_RANK:n_conv + Q_RANK + KV_RANK + LANES]
    rc, rsa, rsb = rc_ref[...], rsa_ref[...], rsb_ref[...]
    q = jnp.dot(_rms(pq, qg_ref[...]).astype(BF16), wuq_ref[...], preferred_element_type=F32) * QK_COEF
    ckv = _rms(pkv, kvg_ref[...]).astype(BF16)
    kn = jnp.dot(ckv, wuk_ref[...], preferred_element_type=F32)
    kr = _rope128(pkr, rc, rsa, rsb).astype(BF16)
    vt = lax.dot_general(wuvt_ref[...], ckv, (((1,), (1,)), ((), ())), preferred_element_type=F32)
    for hd in range(HEADS):
        o = hd * HEAD_SLOT
        q_ref[0, :, o:o + NOPE] = q[:, o:o + NOPE].astype(BF16)
        q_ref[0, :, o + NOPE:o + HEAD_SLOT] = _rope128(q[:, o + NOPE:o + HEAD_SLOT], rc, rsa, rsb).astype(BF16)
        k_ref[0, :, o:o + NOPE] = kn[:, hd * NOPE:(hd + 1) * NOPE].astype(BF16)
        k_ref[0, :, o + NOPE:o + HEAD_SLOT] = kr
        vt_ref[0, hd, 0] = vt[hd * VDIM:(hd + 1) * VDIM, :].astype(BF16)


def _even_in(x, modl, g, w, qg, kvg, wuq, wuk, wuvt, rc, rsa, rsb, n_lat_tiles):
    b, s, d = x.shape
    nt = s // TILE
    tile = lambda n: pl.BlockSpec((1, TILE, n), lambda bb, i: (bb, i, 0))
    rope = pl.BlockSpec((TILE, LANES), lambda bb, i: (i, 0))
    return pl.pallas_call(
        _even_in_body,
        grid=(b, nt),
        in_specs=[tile(d), _mod_spec(0, n_lat_tiles, d), _mod_spec(1, n_lat_tiles, d), _resident((1, d)),
                  _resident(w.shape), _resident(qg.shape), _resident(kvg.shape), _resident(wuq.shape),
                  _resident(wuk.shape), _resident(wuvt.shape), rope, rope, rope],
        out_specs=[tile(2 * CONV_CH), tile(HEADS * HEAD_SLOT), tile(HEADS * HEAD_SLOT),
                   pl.BlockSpec((1, HEADS, 1, VDIM, TILE), lambda bb, i: (bb, 0, i, 0, 0))],
        out_shape=[jax.ShapeDtypeStruct((b, s, 2 * CONV_CH), BF16),
                   jax.ShapeDtypeStruct((b, s, HEADS * HEAD_SLOT), BF16),
                   jax.ShapeDtypeStruct((b, s, HEADS * HEAD_SLOT), BF16),
                   jax.ShapeDtypeStruct((b, HEADS, nt, VDIM, TILE), BF16)],
        compiler_params=_cparams(("arbitrary", "arbitrary"), 48),
        name="even_in",
    )(x, modl, modl, g, w, qg, kvg, wuq, wuk, wuvt, rc, rsa, rsb)


def _glu(v):
    return v[:, :CONV_CH].astype(F32) * jax.nn.sigmoid(v[:, CONV_CH:].astype(F32))


def _conv_body(cur_ref, prev_ref, next_ref, w_ref, b_ref, gg_ref, gb_ref, o_ref, ubuf, shift_s,
               *, n_lat_tiles, n_tiles):
    i = pl.program_id(1)
    left_ok = jnp.logical_and(i != 0, i != n_lat_tiles)
    right_ok = jnp.logical_and(i != n_lat_tiles - 1, i != n_tiles - 1)
    ubuf[0:HALO, :] = jnp.where(left_ok, _glu(prev_ref[0]), 0.0)
    ubuf[HALO:HALO + TILE, :] = _glu(cur_ref[0])
    ubuf[HALO + TILE:2 * HALO + TILE, :] = jnp.where(right_ok, _glu(next_ref[0]), 0.0)
    pad = CONV_WIDTH // 2
    gsz = CONV_CH // CONV_GROUPS
    rows = 64
    sh_rows = shift_s.shape[1]
    for grp in range(CONV_GROUPS):
        ls = slice(grp * gsz, (grp + 1) * gsz)
        for sft in range(1, SUBLANES):
            shift_s[sft - 1] = ubuf[sft:sft + sh_rows, ls]
        for r0 in range(0, TILE, rows):
            acc = jnp.zeros((rows, gsz), F32)
            for kk in range(CONV_WIDTH):
                blk, sft = divmod(HALO - pad + kk, SUBLANES)
                o = blk * SUBLANES + r0
                src = ubuf[o:o + rows, ls] if sft == 0 else shift_s[sft - 1, o:o + rows, :]
                acc = acc + w_ref[kk:kk + 1, ls] * src
            y = acc + b_ref[:, ls]
            mu = jnp.mean(y, axis=-1, keepdims=True)
            yc = y - mu
            var = jnp.mean(yc * yc, axis=-1, keepdims=True)
            yn = yc * lax.rsqrt(var + NORM_EPS) * gg_ref[:, ls] + gb_ref[:, ls]
            o_ref[0, r0:r0 + rows, ls] = (yn * jax.nn.sigmoid(yn)).astype(BF16)


def _conv_branch(pc, w, bias, gg, gb, n_lat_tiles):
    b, s, _ = pc.shape
    nt = s // TILE
    hb = TILE // HALO
    nh = s // HALO
    return pl.pallas_call(
        functools.partial(_conv_body, n_lat_tiles=n_lat_tiles, n_tiles=nt),
        grid=(b, nt),
        in_specs=[pl.BlockSpec((1, TILE, 2 * CONV_CH), lambda bb, i: (bb, i, 0)),
                  pl.BlockSpec((1, HALO, 2 * CONV_CH), lambda bb, i: (bb, jnp.maximum(i * hb - 1, 0), 0)),
                  pl.BlockSpec((1, HALO, 2 * CONV_CH), lambda bb, i: (bb, jnp.minimum((i + 1) * hb, nh - 1), 0)),
                  _resident(w.shape), _resident(bias.shape), _resident(gg.shape), _resident(gb.shape)],
        out_specs=pl.BlockSpec((1, TILE, CONV_CH), lambda bb, i: (bb, i, 0)),
        out_shape=jax.ShapeDtypeStruct((b, s, CONV_CH), BF16),
        scratch_shapes=[pltpu.VMEM((TILE + 2 * HALO, CONV_CH), F32),
                        pltpu.VMEM((SUBLANES - 1, TILE + 2 * HALO - SUBLANES, CONV_CH // CONV_GROUPS), F32)],
        compiler_params=_cparams(("arbitrary", "arbitrary"), 32),
        name="conv_branch",
    )(pc, pc, pc, w, bias, gg, gb)


ATTN_RING = 4


def _attn_body(q_ref, k_ref, v_ref, o_ref, *scratch, n_chunks):
    bufs = tuple(scratch[4 * r:4 * r + 4] for r in range(ATTN_RING))
    m_s, l_s, acc_s = scratch[4 * ATTN_RING:]
    q = q_ref[0]
    m_s[...] = jnp.full(m_s.shape, -jnp.inf, F32)
    l_s[...] = jnp.zeros(l_s.shape, F32)
    acc_s[...] = jnp.zeros(acc_s.shape, F32)
    ring = len(bufs)
    lag = ring // 2

    def scores(j, buf):
        kc = k_ref[0, pl.ds(pl.multiple_of(j * TILE, TILE), TILE), :]
        st = lax.dot_general(kc, q, (((1,), (1,)), ((), ())), preferred_element_type=F32)
        buf[0][...] = st
        buf[3][...] = jnp.max(st, axis=0, keepdims=True)

    def softmax(buf):
        st = buf[0][...]
        m_old = m_s[...]
        m_new = jnp.maximum(m_old, buf[3][...])
        alpha = jnp.exp2(m_old - m_new)
        p = jnp.exp2(st - m_new)
        l_s[...] = alpha * l_s[...] + jnp.sum(p, axis=0, keepdims=True)
        m_s[...] = m_new
        buf[1][...] = p.astype(BF16)
        buf[2][...] = alpha

    def weighted_values(j, buf):
        acc_s[...] = buf[2][...] * acc_s[...] + jnp.dot(v_ref[0, 0, j], buf[1][...], preferred_element_type=F32)

    def stage(t, t_static):
        if 0 <= t_static + lag < n_chunks or not isinstance(t, int):
            scores(t + lag, bufs[(t_static + lag) % ring])
        if 0 <= t_static < n_chunks or not isinstance(t, int):
            softmax(bufs[t_static % ring])
        if 0 <= t_static - lag < n_chunks or not isinstance(t, int):
            weighted_values(t - lag, bufs[(t_static - lag) % ring])

    t_full0 = lag
    n_full = max(n_chunks - 2 * lag, 0)
    n_body = n_full // ring
    for t in range(-lag, min(t_full0, n_chunks + lag)):
        stage(t, t)

    def body(i, carry):
        t0 = t_full0 + ring * i
        for r in range(ring):
            stage(t0 + r, t_full0 + r)
        return carry

    lax.fori_loop(0, n_body, body, 0)
    for t in range(t_full0 + n_body * ring, n_chunks + lag):
        if t >= t_full0:
            stage(t, t)
    o_ref[0] = (acc_s[...] / l_s[...]).T.astype(BF16)


def _attention(q, k, vt, *, tq, q_tile0, n_q_tiles, kv_row0, n_chunks):
    b, s, _ = q.shape
    kv_rows = n_chunks * TILE
    kb = kv_row0 // kv_rows
    cb = kv_row0 // TILE // n_chunks
    return pl.pallas_call(
        functools.partial(_attn_body, n_chunks=n_chunks),
        grid=(b, HEADS, n_q_tiles),
        in_specs=[pl.BlockSpec((1, tq, HEAD_SLOT), lambda bb, h, i: (bb, q_tile0 + i, h)),
                  pl.BlockSpec((1, kv_rows, HEAD_SLOT), lambda bb, h, i: (bb, kb, h)),
                  pl.BlockSpec((1, 1, n_chunks, VDIM, TILE), lambda bb, h, i: (bb, h, cb, 0, 0))],
        out_specs=pl.BlockSpec((1, tq, VDIM), lambda bb, h, i: (bb, i, h)),
        out_shape=jax.ShapeDtypeStruct((b, n_q_tiles * tq, HEADS * VDIM), BF16),
        scratch_shapes=[pltpu.VMEM((TILE, tq), F32), pltpu.VMEM((TILE, tq), BF16),
                        pltpu.VMEM((1, tq), F32), pltpu.VMEM((1, tq), F32)] * ATTN_RING
        + [pltpu.VMEM((1, tq), F32), pltpu.VMEM((1, tq), F32), pltpu.VMEM((VDIM, tq), F32)],
        compiler_params=_cparams(("arbitrary", "arbitrary", "arbitrary"), 58),
        name="attention",
    )(q, k, vt)


def _gelu_tanh(x):
    return 0.5 * x * (1.0 + jnp.tanh(math.sqrt(2.0 / math.pi) * (x + 0.044715 * (x * x * x))))


def _odd_in_body(x_ref, sh_ref, sc_ref, g_ref, w_ref, lg_ref, lb_ref, ws_ref, bst_ref, o_ref):
    h = _rms(x_ref[0], g_ref[...]) * (1.0 + sc_ref[0]) + sh_ref[0]
    z = _gelu_tanh(jnp.dot(h.astype(BF16), w_ref[...], preferred_element_type=F32))
    u = z[:, :GMLP_CH]
    v = z[:, GMLP_CH:]
    mu = jnp.mean(v, axis=-1, keepdims=True)
    vc = v - mu
    var = jnp.mean(vc * vc, axis=-1, keepdims=True)
    vn = (vc * lax.rsqrt(var + NORM_EPS) * lg_ref[...] + lb_ref[...]).astype(BF16)
    gsz = GMLP_CH // GMLP_GROUPS
    for c0 in range(0, TILE, CHUNK):
        for grp in range(GMLP_GROUPS):
            ls = slice(grp * gsz, (grp + 1) * gsz)
            sg = jnp.dot(ws_ref[grp], vn[c0:c0 + CHUNK, ls], preferred_element_type=F32) + bst_ref[:, grp:grp + 1]
            o_ref[0, c0:c0 + CHUNK, ls] = (u[c0:c0 + CHUNK, ls] * sg).astype(BF16)


def _odd_in(x, modl, g, w, lg, lb, ws, bst, n_lat_tiles):
    b, s, d = x.shape
    nt = s // TILE
    return pl.pallas_call(
        _odd_in_body,
        grid=(b, nt),
        in_specs=[pl.BlockSpec((1, TILE, d), lambda bb, i: (bb, i, 0)),
                  _mod_spec(0, n_lat_tiles, d), _mod_spec(1, n_lat_tiles, d), _resident((1, d)),
                  _resident(w.shape), _resident(lg.shape), _resident(lb.shape), _resident(ws.shape),
                  _resident(bst.shape)],
        out_specs=pl.BlockSpec((1, TILE, GMLP_CH), lambda bb, i: (bb, i, 0)),
        out_shape=jax.ShapeDtypeStruct((b, s, GMLP_CH), BF16),
        compiler_params=_cparams(("arbitrary", "arbitrary"), 56),
        name="odd_in",
    )(x, modl, modl, g, w, lg, lb, ws, bst)


def _out_body(*refs, n_parts):
    a_refs = refs[:n_parts]
    w_ref, x_ref, g1_ref, n2_ref, sh_ref, sc_ref, wrt_ref, xo_ref, aff_ref = refs[n_parts:]
    y = None
    off = 0
    for a_ref in a_refs:
        kk = a_ref.shape[-1]
        t = jnp.dot(a_ref[0], w_ref[off:off + kk, :], preferred_element_type=F32)
        y = t if y is None else y + t
        off += kk
    xn = x_ref[0] + g1_ref[0] * y
    xo_ref[0] = xn
    h2 = _rms(xn, n2_ref[...]) * (1.0 + sc_ref[0]) + sh_ref[0]
    lg = lax.dot_general(wrt_ref[...], h2.astype(BF16), (((1,), (1,)), ((), ())), preferred_element_type=F32)
    e = jnp.exp(lg - jnp.max(lg, axis=0, keepdims=True))
    aff_ref[0] = e / jnp.sum(e, axis=0, keepdims=True)


def _out_proj(parts, w, x, modl, n2, wrt, n_lat_tiles):
    b, s, d = x.shape
    nt = s // TILE
    tile = lambda n: pl.BlockSpec((1, TILE, n), lambda bb, i: (bb, i, 0))
    return pl.pallas_call(
        functools.partial(_out_body, n_parts=len(parts)),
        grid=(b, nt),
        in_specs=[tile(a.shape[-1]) for a in parts]
        + [_resident(w.shape), tile(d), _mod_spec(2, n_lat_tiles, d), _resident((1, d)),
           _mod_spec(3, n_lat_tiles, d), _mod_spec(4, n_lat_tiles, d), _resident(wrt.shape)],
        out_specs=[tile(d), pl.BlockSpec((1, N_EXPERTS, TILE), lambda bb, i: (bb, 0, i))],
        out_shape=[jax.ShapeDtypeStruct((b, s, d), F32), jax.ShapeDtypeStruct((b, N_EXPERTS, s), F32)],
        compiler_params=_cparams(("arbitrary", "arbitrary"), 40),
        name="out_proj",
    )(*parts, w, x, modl, n2, modl, modl, wrt)


def _count(pred):
    return jnp.sum(pred.astype(F32), axis=1, keepdims=True)


def _cumsum_lanes(x):
    r, n = x.shape
    tri = (lax.broadcasted_iota(I32, (LANES, LANES), 0) <= lax.broadcasted_iota(I32, (LANES, LANES), 1)).astype(BF16)
    outs = []
    off = jnp.zeros((r, 1), F32)
    for blk in range(n // LANES):
        loc = jnp.dot(x[:, blk * LANES:(blk + 1) * LANES].astype(BF16), tri, preferred_element_type=F32) + off
        outs.append(loc)
        off = loc[:, LANES - 1:LANES]
    return jnp.concatenate(outs, axis=1)


def _split3(x):
    hi = x.astype(BF16).astype(F32)
    r1 = x - hi
    mid = r1.astype(BF16).astype(F32)
    lo = (r1 - mid).astype(BF16).astype(F32)
    return hi, mid, lo


def _topk_body(aff_ref, sel_ref, sx_ref, tc_ref, mask_s, pm_s, planes_s, res_s, *, segs):
    s_total = aff_ref.shape[-1]
    for off, n, cap, _ in segs:
        a = aff_ref[0, :, off:off + n]
        bits = pltpu.bitcast(a, I32)
        thr = jnp.zeros((N_EXPERTS, 1), I32)
        for bit in range(30, -1, -1):
            cand = thr | (1 << bit)
            thr = jnp.where(_count(bits >= cand) >= cap, cand, thr)
        gt = bits > thr
        eq = bits == thr
        need = cap - _count(gt)
        idx = lax.broadcasted_iota(I32, (N_EXPERTS, n), 1)
        j0 = jnp.zeros((N_EXPERTS, 1), I32)
        for bit in range(n.bit_length() - 1, -1, -1):
            cand = j0 | (1 << bit)
            j0 = jnp.where(_count(jnp.logical_and(eq, idx < cand)) < need, cand, j0)
        m = jnp.logical_or(gt, jnp.logical_and(eq, idx <= j0)).astype(F32)
        mask_s[:, off:off + n] = m
        pm_s[:, off:off + n] = _cumsum_lanes(m) * m

    m_all = mask_s[...]
    tc = jnp.sum(m_all, axis=0, keepdims=True)
    sx = _cumsum_lanes(jnp.broadcast_to(tc, (8, s_total)))[0:1] - tc
    tc_ref[0] = tc
    sx_ref[0] = sx
    lower = (lax.broadcasted_iota(I32, (N_EXPERTS, N_EXPERTS), 1)
             < lax.broadcasted_iota(I32, (N_EXPERTS, N_EXPERTS), 0)).astype(BF16)
    rank = jnp.dot(lower, m_all.astype(BF16), preferred_element_type=F32)
    pos = sx + rank
    tok = lax.broadcasted_iota(I32, (N_EXPERTS, s_total), 1).astype(F32)
    g_hi, g_mid, g_lo = _split3(aff_ref[0])
    tok_hi = jnp.floor(tok * (1.0 / 256.0))
    pos_hi = jnp.floor(pos * (1.0 / 256.0))
    planes = (tok_hi, tok - 256.0 * tok_hi, pos_hi, pos - 256.0 * pos_hi, g_hi, g_mid, g_lo, jnp.zeros_like(tok))
    for r, pln in enumerate(planes):
        planes_s[r * N_EXPERTS:(r + 1) * N_EXPERTS, :] = pln.astype(BF16)

    def per_expert(e, carry):
        for off, n, cap, slot0 in segs:
            blk = min(cap, TILE)
            prow = pm_s[pl.ds(e, 1), off:off + n]
            vals = planes_s[:, off:off + n]
            for c0 in range(0, cap, blk):
                slot = (lax.broadcasted_iota(I32, (blk, n), 0) + (c0 + 1)).astype(F32)
                onehot = (prow == slot).astype(BF16)
                res_s[:, 0:blk] = lax.dot_general(vals, onehot, (((1,), (1,)), ((), ())),
                                                  preferred_element_type=F32)
                for r in range(SEL_ROWS):
                    row = res_s[pl.ds(r * N_EXPERTS + e, 1), :]
                    sel_ref[0, e, r:r + 1, slot0 + c0:slot0 + c0 + blk] = row[:, 0:blk]
        return carry

    lax.fori_loop(0, N_EXPERTS, per_expert, 0)


def _topk(aff_t, segs, capt):
    b, e, s = aff_t.shape
    return pl.pallas_call(
        functools.partial(_topk_body, segs=segs),
        grid=(b,),
        in_specs=[pl.BlockSpec((1, e, s), lambda bb: (bb, 0, 0))],
        out_specs=[pl.BlockSpec((1, e, SEL_ROWS, capt), lambda bb: (bb, 0, 0, 0)),
                   pl.BlockSpec((1, 1, s), lambda bb: (bb, 0, 0)),
                   pl.BlockSpec((1, 1, s), lambda bb: (bb, 0, 0))],
        out_shape=[jax.ShapeDtypeStruct((b, e, SEL_ROWS, capt), F32),
                   jax.ShapeDtypeStruct((b, 1, s), F32), jax.ShapeDtypeStruct((b, 1, s), F32)],
        scratch_shapes=[pltpu.VMEM((e, s), F32), pltpu.VMEM((e, s), F32), pltpu.VMEM((SEL_ROWS * e, s), BF16),
                        pltpu.VMEM((SEL_ROWS * e, TILE), F32)],
        compiler_params=_cparams(("arbitrary",), 48),
        name="topk",
    )(aff_t)


def _pack_bf16_pairs(y):
    half = y.shape[1] // 2
    hi = pltpu.bitcast(y[:, :half].astype(BF16).astype(F32), U32)
    lo = pltpu.bitcast(y[:, half:].astype(BF16).astype(F32), U32)
    return hi | lax.shift_right_logical(lo, jnp.uint32(16))


def _unpack_bf16_pairs(w):
    hi = pltpu.bitcast(w & jnp.uint32(0xFFFF0000), F32).astype(BF16)
    lo = pltpu.bitcast(lax.shift_left(w, jnp.uint32(16)), F32).astype(BF16)
    return hi, lo


def _expert_body(ip_hbm, gate0_ref, gate1_ref, x_hbm, sh0_ref, sc0_ref, sh1_ref, sc1_ref, shc_ref, scc_ref, n2_ref,
                 w1_ref, w3_ref, w2_ref, z_hbm, ip0, ip1, ip_sem, xs_buf, gsem, y_buf, ssem, *, n_b, cap_l, capt):
    e = pl.program_id(0)
    pb = pl.program_id(1)
    n_e = pl.num_programs(0)
    n_pb = n_b // 2
    s0 = (e * n_pb + pb) * 2
    n_steps = n_e * n_b
    b0 = 2 * pb
    wrap = pb + 1 == n_pb
    e2 = jnp.where(wrap, e + 1, e)
    b2 = jnp.where(wrap, 0, b0 + 2)
    ips = (ip0, ip1)
    n_grp = capt // SUBLANES

    def ip_copy(ee, bb, u):
        return pltpu.make_async_copy(ip_hbm.at[bb * n_e + ee], ips[u], ip_sem.at[u])

    def gather_start(u):
        for c in range(capt):
            pltpu.make_async_copy(x_hbm.at[pl.ds(ips[u][c], 1)], xs_buf.at[u, pl.ds(c, 1)], gsem.at[u]).start()

    def scatter_start(u):
        for c in range(capt):
            pltpu.make_async_copy(y_buf.at[pl.ds(c, 1)], z_hbm.at[pl.ds(ips[u][capt + c], 1)], ssem).start()

    def scatter_wait():
        pltpu.make_async_copy(y_buf, z_hbm.at[pl.ds(0, capt)], ssem).wait()

    def compute(u, gate_ref, sh_ref, sc_ref):
        pltpu.make_async_copy(x_hbm.at[pl.ds(0, capt)], xs_buf.at[u], gsem.at[u]).wait()
        xs = xs_buf[u]
        hn = _rms(xs, n2_ref[...])
        is_lat = lax.broadcasted_iota(I32, (capt, 1), 0) < cap_l
        sc = jnp.where(is_lat, sc_ref[0], scc_ref[0])
        sh = jnp.where(is_lat, sh_ref[0], shc_ref[0])
        hb = (hn * (1.0 + sc) + sh).astype(BF16)
        a1 = jnp.dot(hb, w1_ref[0, 0], preferred_element_type=F32)
        a3 = jnp.dot(hb, w3_ref[0, 0], preferred_element_type=F32)
        hid = (a1 * jax.nn.sigmoid(a1) * a3).astype(BF16)
        y = jnp.dot(hid, w2_ref[0, 0], preferred_element_type=F32) * gate_ref[0]
        return _pack_bf16_pairs(y)

    @pl.when(s0 == 0)
    def _():
        ip_copy(e, b0, 0).start()
        ip_copy(e, b0, 0).wait()
        gather_start(0)
        ip_copy(e, b0 + 1, 1).start()

    always = s0 >= 0
    ip_copy(e, b0 + 1, 1).wait()

    @pl.when(always)
    def _():
        gather_start(1)

    packed = compute(0, gate0_ref, sh0_ref, sc0_ref)

    @pl.when(s0 > 0)
    def _():
        scatter_wait()

    y_buf[...] = packed

    @pl.when(always)
    def _():
        scatter_start(0)

    @pl.when(s0 + 2 < n_steps)
    def _():
        ip_copy(e2, b2, 0).start()

    @pl.when(s0 + 2 < n_steps)
    def _():
        ip_copy(e2, b2, 0).wait()
        gather_start(0)

    packed = compute(1, gate1_ref, sh1_ref, sc1_ref)
    scatter_wait()
    y_buf[...] = packed

    @pl.when(always)
    def _():
        scatter_start(1)

    @pl.when(s0 + 3 < n_steps)
    def _():
        ip_copy(e2, b2 + 1, 1).start()

    @pl.when(s0 + 2 == n_steps)
    def _():
        scatter_wait()


def _expert_ffn(ip, gate, x2, modl, n2, w1, w3, w2, layer, bsz, cap_l, capt):
    _, d = x2.shape
    _, n_e, _, f = w1.shape
    assert bsz % 2 == 0 and capt % SUBLANES == 0
    ne = n_e * capt
    any_spec = pl.BlockSpec(memory_space=pl.ANY)
    gate_spec = lambda u: pl.BlockSpec((1, capt, 1), lambda e, pb: ((2 * pb + u) * n_e + e, 0, 0))
    mod_b = lambda u, k: pl.BlockSpec((1, 1, d), lambda e, pb: ((2 * pb + u) * N_MOD + k, 0, 0))
    mod_c = lambda k: pl.BlockSpec((1, 1, d), lambda e, pb: (CTX_MOD_ROW * N_MOD + k, 0, 0))
    return pl.pallas_call(
        functools.partial(_expert_body, n_b=bsz, cap_l=cap_l, capt=capt),
        grid=(n_e, bsz // 2),
        in_specs=[any_spec, gate_spec(0), gate_spec(1), any_spec,
                  mod_b(0, 3), mod_b(0, 4), mod_b(1, 3), mod_b(1, 4), mod_c(3), mod_c(4), _resident((1, d)),
                  pl.BlockSpec((1, 1, d, f), lambda e, pb: (layer, e, 0, 0)),
                  pl.BlockSpec((1, 1, d, f), lambda e, pb: (layer, e, 0, 0)),
                  pl.BlockSpec((1, 1, f, d), lambda e, pb: (layer, e, 0, 0))],
        out_specs=any_spec,
        out_shape=jax.ShapeDtypeStruct((bsz * ne, d // 2), U32),
        scratch_shapes=[pltpu.SMEM((2 * capt,), I32), pltpu.SMEM((2 * capt,), I32), pltpu.SemaphoreType.DMA((2,)),
                        pltpu.VMEM((2, capt, d), F32), pltpu.SemaphoreType.DMA((2,)),
                        pltpu.VMEM((capt, d // 2), U32), pltpu.SemaphoreType.DMA(())],
        compiler_params=_cparams(("arbitrary", "arbitrary"), 60),
        name="expert_ffn",
    )(ip, gate, gate, x2, modl, modl, modl, modl, modl, modl, n2, w1, w3, w2)


def _combine_body(ts_ref, x_ref, sx_ref, tc_ref, g2_ref, fg_ref, z_hbm, o_ref, zbuf, zsem, acc_s,
                  *, n_tiles, ne, final):
    b = pl.program_id(0)
    i = pl.program_id(1)
    n_b = pl.num_programs(0)
    half = acc_s.shape[1] // 2
    grp = TILE // SUBLANES

    def tile_range(bb, ii):
        e_lo = ts_ref[bb * (n_tiles + 1) + ii]
        e_hi = ts_ref[bb * (n_tiles + 1) + ii + 1]
        base = lax.shift_left(lax.shift_right_logical(e_lo, 3), 3)
        return base, lax.shift_right_logical(e_hi - base + (TILE - 1), TILE.bit_length() - 1)

    def chunk_start(base, k):
        return jnp.minimum(base + k * TILE, ne - TILE)

    def chunk_copy(bb, base, k, sl):
        g0 = lax.shift_right_logical(chunk_start(base, k), 3)
        return pltpu.make_async_copy(z_hbm.at[bb, pl.ds(g0, grp)], zbuf.at[sl], zsem.at[sl])

    base, n_ch = tile_range(b, i)
    acc_s[...] = jnp.zeros(acc_s.shape, F32)
    sx = sx_ref[0]
    end = sx + tc_ref[0]

    def start_head(bb, hbase, hn_ch):
        for k in range(COMBINE_AHEAD):
            @pl.when(hn_ch > k)
            def _():
                chunk_copy(bb, hbase, k, k).start()

    @pl.when(jnp.logical_and(b == 0, i == 0))
    def _():
        start_head(b, base, n_ch)

    def body(k, carry):
        sl = lax.rem(k, COMBINE_AHEAD + 1)

        @pl.when(k + COMBINE_AHEAD < n_ch)
        def _():
            chunk_copy(b, base, k + COMBINE_AHEAD, lax.rem(k + COMBINE_AHEAD, COMBINE_AHEAD + 1)).start()

        chunk_copy(b, base, k, sl).wait()
        ids = (chunk_start(base, k) + lax.broadcasted_iota(I32, (TILE, 1), 0))
        idf = ids.astype(F32)
        own = jnp.logical_and(jnp.logical_and(ids >= base + k * TILE, idf >= sx), idf < end)
        own = own.astype(BF16)
        hi, lo = _unpack_bf16_pairs(zbuf[sl].reshape(TILE, half))
        dims = (((0,), (0,)), ((), ()))
        acc_s[:, :half] += lax.dot_general(own, hi, dims, preferred_element_type=F32)
        acc_s[:, half:] += lax.dot_general(own, lo, dims, preferred_element_type=F32)
        return carry

    lax.fori_loop(0, n_ch, body, 0)

    last_tile = i + 1 == pl.num_programs(1)
    nb = jnp.where(last_tile, b + 1, b)
    ni = jnp.where(last_tile, 0, i + 1)

    @pl.when(nb < n_b)
    def _():
        nbase, nn_ch = tile_range(nb, ni)
        start_head(nb, nbase, nn_ch)

    out = x_ref[0] + g2_ref[0] * acc_s[...]
    o_ref[0] = _rms(out, fg_ref[...]) if final else out


def _combine(ts, x, sx, tc, modl, z, n_lat_tiles, final_g=None):
    bsz, s, d = x.shape
    nt = s // TILE
    nt_out = nt if final_g is None else n_lat_tiles
    ne = z.shape[1] * SUBLANES
    fg = jnp.ones((1, d), F32) if final_g is None else final_g
    grid_spec = pltpu.PrefetchScalarGridSpec(
        num_scalar_prefetch=1,
        grid=(bsz, nt_out),
        in_specs=[pl.BlockSpec((1, TILE, d), lambda b, i, t: (b, i, 0)),
                  pl.BlockSpec((1, 1, TILE), lambda b, i, t: (b, 0, i)),
                  pl.BlockSpec((1, 1, TILE), lambda b, i, t: (b, 0, i)),
                  pl.BlockSpec((1, 1, d), lambda b, i, t: (jnp.where(i >= n_lat_tiles, CTX_MOD_ROW, b) * N_MOD + 5, 0, 0)),
                  pl.BlockSpec((1, d), lambda b, i, t: (0, 0)),
                  pl.BlockSpec(memory_space=pl.ANY)],
        out_specs=pl.BlockSpec((1, TILE, d), lambda b, i, t: (b, i, 0)),
        scratch_shapes=[pltpu.VMEM((COMBINE_AHEAD + 1, TILE // SUBLANES, SUBLANES, d // 2), U32),
                        pltpu.SemaphoreType.DMA((COMBINE_AHEAD + 1,)),
                        pltpu.VMEM((TILE, d), F32)],
    )
    return pl.pallas_call(
        functools.partial(_combine_body, n_tiles=nt, ne=ne, final=final_g is not None),
        grid_spec=grid_spec,
        out_shape=jax.ShapeDtypeStruct((bsz, nt_out * TILE, d), F32),
        compiler_params=_cparams(("arbitrary", "arbitrary"), 32),
        name="combine",
    )(ts, x, sx, tc, modl, fg, z)


def _rope_tables(n_lat, n_ctx):
    rows = n_lat // GRID_W
    row, col = jnp.meshgrid(jnp.arange(rows, dtype=F32), jnp.arange(GRID_W, dtype=F32), indexing="ij")
    row, col = row.reshape(-1), col.reshape(-1)
    n_freq = ROPE // 4
    inv_freq = ROPE_BASE ** (-jnp.arange(n_freq, dtype=F32) / n_freq)
    ang = jnp.concatenate([row[:, None] * inv_freq, col[:, None] * inv_freq], axis=-1)
    cos, sin = jnp.cos(ang), jnp.sin(ang)
    half = ROPE // 2
    zero = lambda n: jnp.zeros((n_lat, n), F32)
    rc = jnp.concatenate([cos, cos, jnp.ones((n_lat, LANES - ROPE), F32)], axis=1)
    rsa = jnp.concatenate([-sin, zero(LANES - half)], axis=1)
    rsb = jnp.concatenate([zero(half), sin, zero(LANES - ROPE)], axis=1)
    ident = jnp.ones((n_ctx, LANES), F32)
    none = jnp.zeros((n_ctx, LANES), F32)
    return (jnp.concatenate([rc, ident]), jnp.concatenate([rsa, none]), jnp.concatenate([rsb, none]))


def kernel(x, c, ctx, c_ctx, w_mod, b_mod, norm1_g, norm2_g, ev_w_in, ev_q_g, ev_kv_g, ev_w_uq, ev_w_ukv, ev_conv_w, ev_conv_b, ev_gn_g, ev_gn_b, ev_w_out, od_w_in, od_ln_g, od_ln_b, od_w_s, od_b_s, od_w_out, moe_w_r, moe_w1, moe_w3, moe_w2, final_g):
    bsz, n_lat, d = x.shape
    n_ctx = ctx.shape[1]
    depth = w_mod.shape[0]
    lat_tq = min(LAT_TQ, n_lat)
    assert bsz <= CTX_MOD_ROW and n_lat % lat_tq == 0 and n_ctx == TILE and d % LANES == 0
    n_lat_tiles = n_lat // TILE
    cap_l = CAPACITY_FACTOR * n_lat // N_EXPERTS
    cap_c = CAPACITY_FACTOR * n_ctx // N_EXPERTS
    capt = cap_l + cap_c
    segs = ((0, n_lat, cap_l, 0), (n_lat, n_ctx, cap_c, cap_l))

    cc = jnp.zeros((MOD_ROWS, d), F32).at[:bsz].set(c).at[CTX_MOD_ROW].set(c_ctx)
    mod = _modulation(cc, w_mod, b_mod)
    xs = jnp.concatenate([x, ctx], axis=1)
    rc, rsa, rsb = _rope_tables(n_lat, n_ctx)
    row2 = lambda v: v.reshape(1, -1)
    w1_all, w3_all, w2_all = moe_w1.astype(BF16), moe_w3.astype(BF16), moe_w2.astype(BF16)

    for layer in range(depth):
        i = layer // 2
        modl = mod[layer].reshape(MOD_ROWS * N_MOD, 1, d)
        if layer % 2 == 0:
            w = ev_w_in[i]
            conv_off = Q_RANK + KV_RANK + ROPE
            w_e = jnp.concatenate([w[:, conv_off:], w[:, :conv_off],
                                   jnp.zeros((d, IN_EVEN - w.shape[1]), F32)], axis=1).astype(BF16)
            wuq = jnp.pad(ev_w_uq[i].reshape(Q_RANK, HEADS, NOPE + ROPE),
                          ((0, 0), (0, 0), (0, HEAD_SLOT - NOPE - ROPE))).reshape(Q_RANK, HEADS * HEAD_SLOT).astype(BF16)
            wukv = ev_w_ukv[i].reshape(KV_RANK, HEADS, NOPE + VDIM)
            wuk = wukv[:, :, :NOPE].reshape(KV_RANK, HEADS * NOPE).astype(BF16)
            wuvt = wukv[:, :, NOPE:].reshape(KV_RANK, HEADS * VDIM).T.astype(BF16)
            pc, q, k, vt = _even_in(xs, modl, row2(norm1_g[layer]), w_e, row2(ev_q_g[i]), row2(ev_kv_g[i]),
                                    wuq, wuk, wuvt, rc, rsa, rsb, n_lat_tiles)
            conv_w = jnp.pad(ev_conv_w[i], ((0, 1), (0, 0)))
            c_out = _conv_branch(pc, conv_w, row2(ev_conv_b[i]), row2(ev_gn_g[i]), row2(ev_gn_b[i]), n_lat_tiles)
            a_lat = _attention(q, k, vt, tq=lat_tq, q_tile0=0, n_q_tiles=n_lat // lat_tq, kv_row0=0,
                               n_chunks=n_lat_tiles + 1)
            a_ctx = _attention(q, k, vt, tq=TILE, q_tile0=n_lat_tiles, n_q_tiles=1, kv_row0=n_lat, n_chunks=1)
            a_out = jnp.concatenate([a_lat, a_ctx], axis=1)
            parts = [a_out, c_out]
            w_out = ev_w_out[i].astype(BF16)
        else:
            gated = _odd_in(xs, modl, row2(norm1_g[layer]), od_w_in[i].astype(BF16), row2(od_ln_g[i]),
                            row2(od_ln_b[i]), od_w_s[i].astype(BF16), od_b_s[i].T, n_lat_tiles)
            parts = [gated]
            w_out = od_w_out[i].astype(BF16)
        x_mid, aff_t = _out_proj(parts, w_out, xs, modl, row2(norm2_g[layer]), moe_w_r[layer].T.astype(BF16),
                                 n_lat_tiles)
        sel, sx, tc = _topk(aff_t, segs, capt)
        idx = (sel[:, :, 0] * 256.0 + sel[:, :, 1]).astype(I32)
        pos = (sel[:, :, 2] * 256.0 + sel[:, :, 3]).astype(I32)
        gate = (sel[:, :, 4] + sel[:, :, 5]) + sel[:, :, 6]
        s_all = n_lat + n_ctx
        n_ent = N_EXPERTS * capt
        bofs = jnp.arange(bsz, dtype=I32)[:, None, None]
        ip = jnp.concatenate([bofs * s_all + idx, bofs * n_ent + pos], axis=-1)
        z = _expert_ffn(ip.reshape(bsz * N_EXPERTS, 2 * capt), gate.reshape(bsz * N_EXPERTS, capt, 1),
                        x_mid.reshape(bsz * s_all, d), modl, row2(norm2_g[layer]),
                        w1_all, w3_all, w2_all, layer, bsz, cap_l, capt)
        ts = jnp.concatenate([sx[:, 0, ::TILE], sx[:, 0, -1:] + tc[:, 0, -1:]], axis=1).astype(I32).reshape(-1)
        xs = _combine(ts, x_mid, sx, tc, modl, z.reshape(bsz, n_ent // SUBLANES, SUBLANES, d // 2), n_lat_tiles,
                      final_g=row2(final_g) if layer == depth - 1 else None)
    return xs
```

```python
import functools
import math

import jax
import jax.numpy as jnp
from jax import lax
from jax.experimental import pallas as pl
from jax.experimental.pallas import tpu as pltpu

F32 = jnp.float32
BF16 = jnp.bfloat16
I32 = jnp.int32
U32 = jnp.uint32

NORM_EPS = 1e-6
GRID_W = 64
ROPE_BASE = 10000.0
HEADS = 8
NOPE = 128
ROPE = 64
VDIM = 128
VT_ROWS = 144
Q_RANK = 512
KV_RANK = 256
CONV_CH = 1024
CONV_GROUPS = 8
CONV_WIDTH = 31
GMLP_CH = 2048
GMLP_GROUPS = 8
CHUNK = 128
N_EXPERTS = 16
CAPACITY_FACTOR = 2

LANES = 128
TILE = 256
HALO = 16
HEAD_SLOT = 256
N_MOD = 6
CTX_MOD_ROW = 4
MOD_ROWS = 8
IN_EVEN = 3072
SEL_ROWS = 8
QK_COEF = (NOPE + ROPE) ** -0.5 * math.log2(math.e)
SUBLANES = 8
LAT_TQ = 4096
COMBINE_AHEAD = 2


def _cparams(sem, vmem_mb, flags=None):
    return pltpu.CompilerParams(dimension_semantics=sem, vmem_limit_bytes=vmem_mb << 20, flags=flags)


def _resident(shape):
    nd = len(shape)
    return pl.BlockSpec(shape, lambda *_: (0,) * nd, pipeline_mode=pl.Buffered(1))


def _rms(x, g):
    return x * lax.rsqrt(jnp.mean(x * x, axis=-1, keepdims=True) + NORM_EPS) * g


def _mod_spec(k, n_lat_tiles, d):
    return pl.BlockSpec((1, 1, d), lambda b, i: (jnp.where(i >= n_lat_tiles, CTX_MOD_ROW, b) * N_MOD + k, 0, 0))


def _mod_body(c_ref, w_ref, b_ref, o_ref):
    a = c_ref[...]
    a = a * jax.nn.sigmoid(a)
    o_ref[0] = jnp.dot(a.astype(BF16), w_ref[0].astype(BF16), preferred_element_type=F32) + b_ref[0]


def _modulation(cc, w_mod, b_mod):
    n_layers, d, n = w_mod.shape
    tn = max(t for t in (1024, 512, 256, LANES) if n % t == 0)
    return pl.pallas_call(
        _mod_body,
        grid=(n_layers, n // tn),
        in_specs=[pl.BlockSpec((MOD_ROWS, d), lambda l, j: (0, 0)),
                  pl.BlockSpec((1, d, tn), lambda l, j: (l, 0, j)),
                  pl.BlockSpec((1, 1, tn), lambda l, j: (l, 0, j))],
        out_specs=pl.BlockSpec((1, MOD_ROWS, tn), lambda l, j: (l, 0, j)),
        out_shape=jax.ShapeDtypeStruct((n_layers, MOD_ROWS, n), F32),
        compiler_params=_cparams(("arbitrary", "arbitrary"), 40),
        name="modulation",
    )(cc, w_mod, b_mod.reshape(n_layers, 1, n))


def _rope128(grp, c, sa, sb):
    return grp * c + pltpu.roll(grp, 96, 1) * sa + pltpu.roll(grp, 32, 1) * sb


def _even_in_body(x_ref, sh_ref, sc_ref, g_ref, w_ref, qg_ref, kvg_ref, wuq_ref, wuk_ref, wuvt_ref,
                  rc_ref, rsa_ref, rsb_ref, pc_ref, q_ref, k_ref, vt_ref):
    h = _rms(x_ref[0], g_ref[...]) * (1.0 + sc_ref[0]) + sh_ref[0]
    p = jnp.dot(h.astype(BF16), w_ref[...], preferred_element_type=F32)
    n_conv = 2 * CONV_CH
    pc_ref[0] = p[:, :n_conv].astype(BF16)
    pq = p[:, n_conv:n_conv + Q_RANK]
    pkv = p[:, n_conv + Q_RANK:n_conv + Q_RANK + KV_RANK]
    pkr = p[:, n_conv + Q_RANK + KV_RANK:n_conv + Q_RANK + KV_RANK + LANES]
    rc, rsa, rsb = rc_ref[...], rsa_ref[...], rsb_ref[...]
    q = jnp.dot(_rms(pq, qg_ref[...]).astype(BF16), wuq_ref[...], preferred_element_type=F32) * QK_COEF
    ckv = _rms(pkv, kvg_ref[...]).astype(BF16)
    kn = jnp.dot(ckv, wuk_ref[...], preferred_element_type=F32)
    kr = _rope128(pkr, rc, rsa, rsb).astype(BF16)
    vt = lax.dot_general(wuvt_ref[...], ckv, (((1,), (1,)), ((), ())), preferred_element_type=F32)
    for hd in range(HEADS):
        o = hd * HEAD_SLOT
        q_ref[0, :, o:o + NOPE] = q[:, o:o + NOPE].astype(BF16)
        q_ref[0, :, o + NOPE:o + HEAD_SLOT] = _rope128(q[:, o + NOPE:o + HEAD_SLOT], rc, rsa, rsb).astype(BF16)
        k_ref[0, :, o:o + NOPE] = kn[:, hd * NOPE:(hd + 1) * NOPE].astype(BF16)
        k_ref[0, :, o + NOPE:o + HEAD_SLOT] = kr
        extra = (lax.broadcasted_iota(I32, (VT_ROWS - VDIM, TILE), 0) == 0).astype(F32)
        vt_ref[0, hd, 0] = jnp.concatenate([vt[hd * VDIM:(hd + 1) * VDIM, :], extra], axis=0).astype(BF16)


def _even_in(x, modl, g, w, qg, kvg, wuq, wuk, wuvt, rc, rsa, rsb, n_lat_tiles):
    b, s, d = x.shape
    nt = s // TILE
    tile = lambda n: pl.BlockSpec((1, TILE, n), lambda bb, i: (bb, i, 0))
    rope = pl.BlockSpec((TILE, LANES), lambda bb, i: (i, 0))
    return pl.pallas_call(
        _even_in_body,
        grid=(b, nt),
        in_specs=[tile(d), _mod_spec(0, n_lat_tiles, d), _mod_spec(1, n_lat_tiles, d), _resident((1, d)),
                  _resident(w.shape), _resident(qg.shape), _resident(kvg.shape), _resident(wuq.shape),
                  _resident(wuk.shape), _resident(wuvt.shape), rope, rope, rope],
        out_specs=[tile(2 * CONV_CH), tile(HEADS * HEAD_SLOT), tile(HEADS * HEAD_SLOT),
                   pl.BlockSpec((1, HEADS, 1, VT_ROWS, TILE), lambda bb, i: (bb, 0, i, 0, 0))],
        out_shape=[jax.ShapeDtypeStruct((b, s, 2 * CONV_CH), BF16),
                   jax.ShapeDtypeStruct((b, s, HEADS * HEAD_SLOT), BF16),
                   jax.ShapeDtypeStruct((b, s, HEADS * HEAD_SLOT), BF16),
                   jax.ShapeDtypeStruct((b, HEADS, nt, VT_ROWS, TILE), BF16)],
        compiler_params=_cparams(("arbitrary", "arbitrary"), 48),
        name="even_in",
    )(x, modl, modl, g, w, qg, kvg, wuq, wuk, wuvt, rc, rsa, rsb)


def _glu(v):
    return v[:, :CONV_CH].astype(F32) * jax.nn.sigmoid(v[:, CONV_CH:].astype(F32))


def _conv_body(cur_ref, prev_ref, next_ref, w_ref, b_ref, gg_ref, gb_ref, o_ref, ubuf, shift_s,
               *, n_lat_tiles, n_tiles):
    i = pl.program_id(1)
    left_ok = jnp.logical_and(i != 0, i != n_lat_tiles)
    right_ok = jnp.logical_and(i != n_lat_tiles - 1, i != n_tiles - 1)
    ubuf[0:HALO, :] = jnp.where(left_ok, _glu(prev_ref[0]), 0.0)
    ubuf[HALO:HALO + TILE, :] = _glu(cur_ref[0])
    ubuf[HALO + TILE:2 * HALO + TILE, :] = jnp.where(right_ok, _glu(next_ref[0]), 0.0)
    pad = CONV_WIDTH // 2
    gsz = CONV_CH // CONV_GROUPS
    rows = 64
    sh_rows = shift_s.shape[1]
    for grp in range(CONV_GROUPS):
        ls = slice(grp * gsz, (grp + 1) * gsz)
        for sft in range(1, SUBLANES):
            shift_s[sft - 1] = ubuf[sft:sft + sh_rows, ls]
        for r0 in range(0, TILE, rows):
            acc = jnp.zeros((rows, gsz), F32)
            for kk in range(CONV_WIDTH):
                blk, sft = divmod(HALO - pad + kk, SUBLANES)
                o = blk * SUBLANES + r0
                src = ubuf[o:o + rows, ls] if sft == 0 else shift_s[sft - 1, o:o + rows, :]
                acc = acc + w_ref[kk:kk + 1, ls] * src
            y = acc + b_ref[:, ls]
            mu = jnp.mean(y, axis=-1, keepdims=True)
            yc = y - mu
            var = jnp.mean(yc * yc, axis=-1, keepdims=True)
            yn = yc * lax.rsqrt(var + NORM_EPS) * gg_ref[:, ls] + gb_ref[:, ls]
            o_ref[0, r0:r0 + rows, ls] = (yn * jax.nn.sigmoid(yn)).astype(BF16)


def _conv_branch(pc, w, bias, gg, gb, n_lat_tiles):
    b, s, _ = pc.shape
    nt = s // TILE
    hb = TILE // HALO
    nh = s // HALO
    return pl.pallas_call(
        functools.partial(_conv_body, n_lat_tiles=n_lat_tiles, n_tiles=nt),
        grid=(b, nt),
        in_specs=[pl.BlockSpec((1, TILE, 2 * CONV_CH), lambda bb, i: (bb, i, 0)),
                  pl.BlockSpec((1, HALO, 2 * CONV_CH), lambda bb, i: (bb, jnp.maximum(i * hb - 1, 0), 0)),
                  pl.BlockSpec((1, HALO, 2 * CONV_CH), lambda bb, i: (bb, jnp.minimum((i + 1) * hb, nh - 1), 0)),
                  _resident(w.shape), _resident(bias.shape), _resident(gg.shape), _resident(gb.shape)],
        out_specs=pl.BlockSpec((1, TILE, CONV_CH), lambda bb, i: (bb, i, 0)),
        out_shape=jax.ShapeDtypeStruct((b, s, CONV_CH), BF16),
        scratch_shapes=[pltpu.VMEM((TILE + 2 * HALO, CONV_CH), F32),
                        pltpu.VMEM((SUBLANES - 1, TILE + 2 * HALO - SUBLANES, CONV_CH // CONV_GROUPS), F32)],
        compiler_params=_cparams(("arbitrary", "arbitrary"), 32),
        name="conv_branch",
    )(pc, pc, pc, w, bias, gg, gb)


ATTN_RING = 4


def _attn_body(q_ref, k_ref, v_ref, o_ref, *scratch, n_chunks):
    bufs = tuple(scratch[4 * r:4 * r + 4] for r in range(ATTN_RING))
    m_s, acc_s = scratch[4 * ATTN_RING:]
    q = q_ref[0]
    m_s[...] = jnp.full(m_s.shape, -jnp.inf, F32)
    acc_s[...] = jnp.zeros(acc_s.shape, F32)
    ring = len(bufs)
    lag = ring // 2

    def scores(j, buf):
        kc = k_ref[0, pl.ds(pl.multiple_of(j * TILE, TILE), TILE), :]
        st = lax.dot_general(kc, q, (((1,), (1,)), ((), ())), preferred_element_type=F32)
        buf[0][...] = st
        buf[3][...] = jnp.max(st, axis=0, keepdims=True)

    def softmax(buf):
        st = buf[0][...]
        m_old = m_s[...]
        m_new = jnp.maximum(m_old, buf[3][...])
        alpha = jnp.exp2(m_old - m_new)
        p = jnp.exp2(st - m_new)
        m_s[...] = m_new
        buf[1][...] = p.astype(BF16)
        buf[2][...] = alpha

    def weighted_values(j, buf):
        acc_s[...] = buf[2][...] * acc_s[...] + jnp.dot(v_ref[0, 0, j], buf[1][...], preferred_element_type=F32)

    def stage(t, t_static):
        if 0 <= t_static + lag < n_chunks or not isinstance(t, int):
            scores(t + lag, bufs[(t_static + lag) % ring])
        if 0 <= t_static < n_chunks or not isinstance(t, int):
            softmax(bufs[t_static % ring])
        if 0 <= t_static - lag < n_chunks or not isinstance(t, int):
            weighted_values(t - lag, bufs[(t_static - lag) % ring])

    t_full0 = lag
    n_full = max(n_chunks - 2 * lag, 0)
    n_body = n_full // ring
    for t in range(-lag, min(t_full0, n_chunks + lag)):
        stage(t, t)

    def body(i, carry):
        t0 = t_full0 + ring * i
        for r in range(ring):
            stage(t0 + r, t_full0 + r)
        return carry

    lax.fori_loop(0, n_body, body, 0)
    for t in range(t_full0 + n_body * ring, n_chunks + lag):
        if t >= t_full0:
            stage(t, t)
    o_ref[0] = (acc_s[0:VDIM, :] / acc_s[VDIM:VDIM + 1, :]).T.astype(BF16)


def _attention(q, k, vt, *, tq, q_tile0, n_q_tiles, kv_row0, n_chunks):
    b, s, _ = q.shape
    kv_rows = n_chunks * TILE
    kb = kv_row0 // kv_rows
    cb = kv_row0 // TILE // n_chunks
    return pl.pallas_call(
        functools.partial(_attn_body, n_chunks=n_chunks),
        grid=(b, HEADS, n_q_tiles),
        in_specs=[pl.BlockSpec((1, tq, HEAD_SLOT), lambda bb, h, i: (bb, q_tile0 + i, h)),
                  pl.BlockSpec((1, kv_rows, HEAD_SLOT), lambda bb, h, i: (bb, kb, h)),
                  pl.BlockSpec((1, 1, n_chunks, VT_ROWS, TILE), lambda bb, h, i: (bb, h, cb, 0, 0))],
        out_specs=pl.BlockSpec((1, tq, VDIM), lambda bb, h, i: (bb, i, h)),
        out_shape=jax.ShapeDtypeStruct((b, n_q_tiles * tq, HEADS * VDIM), BF16),
        scratch_shapes=[pltpu.VMEM((TILE, tq), F32), pltpu.VMEM((TILE, tq), BF16),
                        pltpu.VMEM((1, tq), F32), pltpu.VMEM((1, tq), F32)] * ATTN_RING
        + [pltpu.VMEM((1, tq), F32), pltpu.VMEM((VT_ROWS, tq), F32)],
        compiler_params=_cparams(("arbitrary", "arbitrary", "arbitrary"), 58),
        name="attention",
    )(q, k, vt)


def _gelu_tanh(x):
    return 0.5 * x * (1.0 + jnp.tanh(math.sqrt(2.0 / math.pi) * (x + 0.044715 * (x * x * x))))


def _odd_in_body(x_ref, sh_ref, sc_ref, g_ref, w_ref, lg_ref, lb_ref, ws_ref, bst_ref, o_ref):
    h = _rms(x_ref[0], g_ref[...]) * (1.0 + sc_ref[0]) + sh_ref[0]
    z = _gelu_tanh(jnp.dot(h.astype(BF16), w_ref[...], preferred_element_type=F32))
    u = z[:, :GMLP_CH]
    v = z[:, GMLP_CH:]
    mu = jnp.mean(v, axis=-1, keepdims=True)
    vc = v - mu
    var = jnp.mean(vc * vc, axis=-1, keepdims=True)
    vn = (vc * lax.rsqrt(var + NORM_EPS) * lg_ref[...] + lb_ref[...]).astype(BF16)
    gsz = GMLP_CH // GMLP_GROUPS
    for c0 in range(0, TILE, CHUNK):
        for grp in range(GMLP_GROUPS):
            ls = slice(grp * gsz, (grp + 1) * gsz)
            sg = jnp.dot(ws_ref[grp], vn[c0:c0 + CHUNK, ls], preferred_element_type=F32) + bst_ref[:, grp:grp + 1]
            o_ref[0, c0:c0 + CHUNK, ls] = (u[c0:c0 + CHUNK, ls] * sg).astype(BF16)


def _odd_in(x, modl, g, w, lg, lb, ws, bst, n_lat_tiles):
    b, s, d = x.shape
    nt = s // TILE
    return pl.pallas_call(
        _odd_in_body,
        grid=(b, nt),
        in_specs=[pl.BlockSpec((1, TILE, d), lambda bb, i: (bb, i, 0)),
                  _mod_spec(0, n_lat_tiles, d), _mod_spec(1, n_lat_tiles, d), _resident((1, d)),
                  _resident(w.shape), _resident(lg.shape), _resident(lb.shape), _resident(ws.shape),
                  _resident(bst.shape)],
        out_specs=pl.BlockSpec((1, TILE, GMLP_CH), lambda bb, i: (bb, i, 0)),
        out_shape=jax.ShapeDtypeStruct((b, s, GMLP_CH), BF16),
        compiler_params=_cparams(("arbitrary", "arbitrary"), 56),
        name="odd_in",
    )(x, modl, modl, g, w, lg, lb, ws, bst)


def _out_body(*refs, n_parts):
    a_refs = refs[:n_parts]
    w_ref, x_ref, g1_ref, n2_ref, sh_ref, sc_ref, wrt_ref, xo_ref, aff_ref = refs[n_parts:]
    y = None
    off = 0
    for a_ref in a_refs:
        kk = a_ref.shape[-1]
        t = jnp.dot(a_ref[0], w_ref[off:off + kk, :], preferred_element_type=F32)
        y = t if y is None else y + t
        off += kk
    xn = x_ref[0] + g1_ref[0] * y
    xo_ref[0] = xn
    h2 = _rms(xn, n2_ref[...]) * (1.0 + sc_ref[0]) + sh_ref[0]
    lg = lax.dot_general(wrt_ref[...], h2.astype(BF16), (((1,), (1,)), ((), ())), preferred_element_type=F32)
    e = jnp.exp(lg - jnp.max(lg, axis=0, keepdims=True))
    aff_ref[0] = e / jnp.sum(e, axis=0, keepdims=True)


def _out_proj(parts, w, x, modl, n2, wrt, n_lat_tiles):
    b, s, d = x.shape
    nt = s // TILE
    tile = lambda n: pl.BlockSpec((1, TILE, n), lambda bb, i: (bb, i, 0))
    return pl.pallas_call(
        functools.partial(_out_body, n_parts=len(parts)),
        grid=(b, nt),
        in_specs=[tile(a.shape[-1]) for a in parts]
        + [_resident(w.shape), tile(d), _mod_spec(2, n_lat_tiles, d), _resident((1, d)),
           _mod_spec(3, n_lat_tiles, d), _mod_spec(4, n_lat_tiles, d), _resident(wrt.shape)],
        out_specs=[tile(d), pl.BlockSpec((1, N_EXPERTS, TILE), lambda bb, i: (bb, 0, i))],
        out_shape=[jax.ShapeDtypeStruct((b, s, d), F32), jax.ShapeDtypeStruct((b, N_EXPERTS, s), F32)],
        compiler_params=_cparams(("arbitrary", "arbitrary"), 40),
        name="out_proj",
    )(*parts, w, x, modl, n2, modl, modl, wrt)


def _count(pred):
    return jnp.sum(pred.astype(F32), axis=1, keepdims=True)


def _cumsum_lanes(x):
    r, n = x.shape
    tri = (lax.broadcasted_iota(I32, (LANES, LANES), 0) <= lax.broadcasted_iota(I32, (LANES, LANES), 1)).astype(BF16)
    outs = []
    off = jnp.zeros((r, 1), F32)
    for blk in range(n // LANES):
        loc = jnp.dot(x[:, blk * LANES:(blk + 1) * LANES].astype(BF16), tri, preferred_element_type=F32) + off
        outs.append(loc)
        off = loc[:, LANES - 1:LANES]
    return jnp.concatenate(outs, axis=1)


def _split3(x):
    hi = x.astype(BF16).astype(F32)
    r1 = x - hi
    mid = r1.astype(BF16).astype(F32)
    lo = (r1 - mid).astype(BF16).astype(F32)
    return hi, mid, lo


def _topk_body(aff_ref, sel_ref, sx_ref, tc_ref, mask_s, pm_s, planes_s, res_s, *, segs):
    s_total = aff_ref.shape[-1]
    for off, n, cap, _ in segs:
        a = aff_ref[0, :, off:off + n]
        bits = pltpu.bitcast(a, I32)
        thr = jnp.zeros((N_EXPERTS, 1), I32)
        for bit in range(30, -1, -1):
            cand = thr | (1 << bit)
            thr = jnp.where(_count(bits >= cand) >= cap, cand, thr)
        gt = bits > thr
        eq = bits == thr
        need = cap - _count(gt)
        idx = lax.broadcasted_iota(I32, (N_EXPERTS, n), 1)
        j0 = jnp.zeros((N_EXPERTS, 1), I32)
        for bit in range(n.bit_length() - 1, -1, -1):
            cand = j0 | (1 << bit)
            j0 = jnp.where(_count(jnp.logical_and(eq, idx < cand)) < need, cand, j0)
        m = jnp.logical_or(gt, jnp.logical_and(eq, idx <= j0)).astype(F32)
        mask_s[:, off:off + n] = m
        pm_s[:, off:off + n] = _cumsum_lanes(m) * m

    m_all = mask_s[...]
    tc = jnp.sum(m_all, axis=0, keepdims=True)
    sx = _cumsum_lanes(jnp.broadcast_to(tc, (8, s_total)))[0:1] - tc
    tc_ref[0] = tc
    sx_ref[0] = sx
    lower = (lax.broadcasted_iota(I32, (N_EXPERTS, N_EXPERTS), 1)
             < lax.broadcasted_iota(I32, (N_EXPERTS, N_EXPERTS), 0)).astype(BF16)
    rank = jnp.dot(lower, m_all.astype(BF16), preferred_element_type=F32)
    pos = sx + rank
    tok = lax.broadcasted_iota(I32, (N_EXPERTS, s_total), 1).astype(F32)
    g_hi, g_mid, g_lo = _split3(aff_ref[0])
    tok_hi = jnp.floor(tok * (1.0 / 256.0))
    pos_hi = jnp.floor(pos * (1.0 / 256.0))
    planes = (tok_hi, tok - 256.0 * tok_hi, pos_hi, pos - 256.0 * pos_hi, g_hi, g_mid, g_lo, jnp.zeros_like(tok))
    for r, pln in enumerate(planes):
        planes_s[r * N_EXPERTS:(r + 1) * N_EXPERTS, :] = pln.astype(BF16)

    def per_expert(e, carry):
        for off, n, cap, slot0 in segs:
            blk = min(cap, TILE)
            prow = pm_s[pl.ds(e, 1), off:off + n]
            vals = planes_s[:, off:off + n]
            for c0 in range(0, cap, blk):
                slot = (lax.broadcasted_iota(I32, (blk, n), 0) + (c0 + 1)).astype(F32)
                onehot = (prow == slot).astype(BF16)
                res_s[:, 0:blk] = lax.dot_general(vals, onehot, (((1,), (1,)), ((), ())),
                                                  preferred_element_type=F32)
                for r in range(SEL_ROWS):
                    row = res_s[pl.ds(r * N_EXPERTS + e, 1), :]
                    sel_ref[0, e, r:r + 1, slot0 + c0:slot0 + c0 + blk] = row[:, 0:blk]
        return carry

    lax.fori_loop(0, N_EXPERTS, per_expert, 0)


def _topk(aff_t, segs, capt):
    b, e, s = aff_t.shape
    return pl.pallas_call(
        functools.partial(_topk_body, segs=segs),
        grid=(b,),
        in_specs=[pl.BlockSpec((1, e, s), lambda bb: (bb, 0, 0))],
        out_specs=[pl.BlockSpec((1, e, SEL_ROWS, capt), lambda bb: (bb, 0, 0, 0)),
                   pl.BlockSpec((1, 1, s), lambda bb: (bb, 0, 0)),
                   pl.BlockSpec((1, 1, s), lambda bb: (bb, 0, 0))],
        out_shape=[jax.ShapeDtypeStruct((b, e, SEL_ROWS, capt), F32),
                   jax.ShapeDtypeStruct((b, 1, s), F32), jax.ShapeDtypeStruct((b, 1, s), F32)],
        scratch_shapes=[pltpu.VMEM((e, s), F32), pltpu.VMEM((e, s), F32), pltpu.VMEM((SEL_ROWS * e, s), BF16),
                        pltpu.VMEM((SEL_ROWS * e, TILE), F32)],
        compiler_params=_cparams(("arbitrary",), 48),
        name="topk",
    )(aff_t)


def _pack_bf16_pairs(y):
    half = y.shape[1] // 2
    hi = pltpu.bitcast(y[:, :half].astype(BF16).astype(F32), U32)
    lo = pltpu.bitcast(y[:, half:].astype(BF16).astype(F32), U32)
    return hi | lax.shift_right_logical(lo, jnp.uint32(16))


def _unpack_bf16_pairs(w):
    hi = pltpu.bitcast(w & jnp.uint32(0xFFFF0000), F32).astype(BF16)
    lo = pltpu.bitcast(lax.shift_left(w, jnp.uint32(16)), F32).astype(BF16)
    return hi, lo


def _expert_body(ip_hbm, gate0_ref, gate1_ref, x_hbm, sh0_ref, sc0_ref, sh1_ref, sc1_ref, shc_ref, scc_ref, n2_ref,
                 w1_ref, w3_ref, w2_ref, z_hbm, ip0, ip1, ip_sem, xs_buf, gsem, y_buf, ssem, *, n_b, cap_l, capt):
    e = pl.program_id(0)
    pb = pl.program_id(1)
    n_e = pl.num_programs(0)
    n_pb = n_b // 2
    s0 = (e * n_pb + pb) * 2
    n_steps = n_e * n_b
    b0 = 2 * pb
    wrap = pb + 1 == n_pb
    e2 = jnp.where(wrap, e + 1, e)
    b2 = jnp.where(wrap, 0, b0 + 2)
    ips = (ip0, ip1)
    n_grp = capt // SUBLANES

    def ip_copy(ee, bb, u):
        return pltpu.make_async_copy(ip_hbm.at[bb * n_e + ee], ips[u], ip_sem.at[u])

    def gather_start(u):
        for c in range(capt):
            pltpu.make_async_copy(x_hbm.at[pl.ds(ips[u][c], 1)], xs_buf.at[u, pl.ds(c, 1)], gsem.at[u]).start()

    def scatter_start(u):
        for c in range(capt):
            pltpu.make_async_copy(y_buf.at[pl.ds(c, 1)], z_hbm.at[pl.ds(ips[u][capt + c], 1)], ssem).start()

    def scatter_wait():
        pltpu.make_async_copy(y_buf, z_hbm.at[pl.ds(0, capt)], ssem).wait()

    def gather_wait(u):
        pltpu.make_async_copy(x_hbm.at[pl.ds(0, capt)], xs_buf.at[u], gsem.at[u]).wait()

    def compute(u, gate_ref, sh_ref, sc_ref):
        xs = xs_buf[u]
        hn = _rms(xs, n2_ref[...])
        is_lat = lax.broadcasted_iota(I32, (capt, 1), 0) < cap_l
        sc = jnp.where(is_lat, sc_ref[0], scc_ref[0])
        sh = jnp.where(is_lat, sh_ref[0], shc_ref[0])
        hb = (hn * (1.0 + sc) + sh).astype(BF16)
        a1 = jnp.dot(hb, w1_ref[0, 0], preferred_element_type=F32)
        a3 = jnp.dot(hb, w3_ref[0, 0], preferred_element_type=F32)
        hid = (a1 * jax.nn.sigmoid(a1) * a3).astype(BF16)
        y = jnp.dot(hid, w2_ref[0, 0], preferred_element_type=F32) * gate_ref[0]
        return _pack_bf16_pairs(y)

    @pl.when(s0 == 0)
    def _():
        ip_copy(e, b0, 0).start()
        ip_copy(e, b0, 0).wait()
        gather_start(0)
        ip_copy(e, b0 + 1, 1).start()

    always = s0 >= 0
    ip_copy(e, b0 + 1, 1).wait()

    @pl.when(always)
    def _():
        gather_start(1)

    gather_wait(0)
    packed = compute(0, gate0_ref, sh0_ref, sc0_ref)

    @pl.when(s0 > 0)
    def _():
        scatter_wait()

    y_buf[...] = packed

    @pl.when(always)
    def _():
        scatter_start(0)

    @pl.when(s0 + 2 < n_steps)
    def _():
        ip_copy(e2, b2, 0).start()

    @pl.when(s0 + 2 < n_steps)
    def _():
        ip_copy(e2, b2, 0).wait()
        gather_start(0)

    gather_wait(1)
    packed = compute(1, gate1_ref, sh1_ref, sc1_ref)
    scatter_wait()
    y_buf[...] = packed

    @pl.when(always)
    def _():
        scatter_start(1)

    @pl.when(s0 + 3 < n_steps)
    def _():
        ip_copy(e2, b2 + 1, 1).start()

    @pl.when(s0 + 2 == n_steps)
    def _():
        scatter_wait()


def _expert_ffn(ip, gate, x2, modl, n2, w1, w3, w2, layer, bsz, cap_l, capt):
    _, d = x2.shape
    _, n_e, _, f = w1.shape
    assert bsz % 2 == 0 and capt % SUBLANES == 0
    ne = n_e * capt
    any_spec = pl.BlockSpec(memory_space=pl.ANY)
    gate_spec = lambda u: pl.BlockSpec((1, capt, 1), lambda e, pb: ((2 * pb + u) * n_e + e, 0, 0))
    mod_b = lambda u, k: pl.BlockSpec((1, 1, d), lambda e, pb: ((2 * pb + u) * N_MOD + k, 0, 0))
    mod_c = lambda k: pl.BlockSpec((1, 1, d), lambda e, pb: (CTX_MOD_ROW * N_MOD + k, 0, 0))
    return pl.pallas_call(
        functools.partial(_expert_body, n_b=bsz, cap_l=cap_l, capt=capt),
        grid=(n_e, bsz // 2),
        in_specs=[any_spec, gate_spec(0), gate_spec(1), any_spec,
                  mod_b(0, 3), mod_b(0, 4), mod_b(1, 3), mod_b(1, 4), mod_c(3), mod_c(4), _resident((1, d)),
                  pl.BlockSpec((1, 1, d, f), lambda e, pb: (layer, e, 0, 0)),
                  pl.BlockSpec((1, 1, d, f), lambda e, pb: (layer, e, 0, 0)),
                  pl.BlockSpec((1, 1, f, d), lambda e, pb: (layer, e, 0, 0))],
        out_specs=any_spec,
        out_shape=jax.ShapeDtypeStruct((bsz * ne, d // 2), U32),
        scratch_shapes=[pltpu.SMEM((2 * capt,), I32), pltpu.SMEM((2 * capt,), I32), pltpu.SemaphoreType.DMA((2,)),
                        pltpu.VMEM((2, capt, d), F32), pltpu.SemaphoreType.DMA((2,)),
                        pltpu.VMEM((capt, d // 2), U32), pltpu.SemaphoreType.DMA(())],
        compiler_params=_cparams(("arbitrary", "arbitrary"), 60),
        name="expert_ffn",
    )(ip, gate, gate, x2, modl, modl, modl, modl, modl, modl, n2, w1, w3, w2)


def _combine_body(ts_ref, x_ref, sx_ref, tc_ref, g2_ref, fg_ref, z_hbm, o_ref, zbuf, zsem, acc_s,
                  *, n_tiles, ne, final):
    b = pl.program_id(0)
    i = pl.program_id(1)
    n_b = pl.num_programs(0)
    half = acc_s.shape[1] // 2
    grp = TILE // SUBLANES

    def tile_range(bb, ii):
        e_lo = ts_ref[bb * (n_tiles + 1) + ii]
        e_hi = ts_ref[bb * (n_tiles + 1) + ii + 1]
        base = lax.shift_left(lax.shift_right_logical(e_lo, 3), 3)
        return base, lax.shift_right_logical(e_hi - base + (TILE - 1), TILE.bit_length() - 1)

    def chunk_start(base, k):
        return jnp.minimum(base + k * TILE, ne - TILE)

    def chunk_copy(bb, base, k, sl):
        g0 = lax.shift_right_logical(chunk_start(base, k), 3)
        return pltpu.make_async_copy(z_hbm.at[bb, pl.ds(g0, grp)], zbuf.at[sl], zsem.at[sl])

    base, n_ch = tile_range(b, i)
    acc_s[...] = jnp.zeros(acc_s.shape, F32)
    sx = sx_ref[0]
    end = sx + tc_ref[0]

    def start_head(bb, hbase, hn_ch):
        for k in range(COMBINE_AHEAD):
            @pl.when(hn_ch > k)
            def _():
                chunk_copy(bb, hbase, k, k).start()

    @pl.when(jnp.logical_and(b == 0, i == 0))
    def _():
        start_head(b, base, n_ch)

    def body(k, carry):
        sl = lax.rem(k, COMBINE_AHEAD + 1)

        @pl.when(k + COMBINE_AHEAD < n_ch)
        def _():
            chunk_copy(b, base, k + COMBINE_AHEAD, lax.rem(k + COMBINE_AHEAD, COMBINE_AHEAD + 1)).start()

        chunk_copy(b, base, k, sl).wait()
        ids = (chunk_start(base, k) + lax.broadcasted_iota(I32, (TILE, 1), 0))
        idf = ids.astype(F32)
        own = jnp.logical_and(jnp.logical_and(ids >= base + k * TILE, idf >= sx), idf < end)
        own = own.astype(BF16)
        hi, lo = _unpack_bf16_pairs(zbuf[sl].reshape(TILE, half))
        dims = (((0,), (0,)), ((), ()))
        acc_s[:, :half] += lax.dot_general(own, hi, dims, preferred_element_type=F32)
        acc_s[:, half:] += lax.dot_general(own, lo, dims, preferred_element_type=F32)
        return carry

    lax.fori_loop(0, n_ch, body, 0)

    last_tile = i + 1 == pl.num_programs(1)
    nb = jnp.where(last_tile, b + 1, b)
    ni = jnp.where(last_tile, 0, i + 1)

    @pl.when(nb < n_b)
    def _():
        nbase, nn_ch = tile_range(nb, ni)
        start_head(nb, nbase, nn_ch)

    out = x_ref[0] + g2_ref[0] * acc_s[...]
    o_ref[0] = _rms(out, fg_ref[...]) if final else out


def _combine(ts, x, sx, tc, modl, z, n_lat_tiles, final_g=None):
    bsz, s, d = x.shape
    nt = s // TILE
    nt_out = nt if final_g is None else n_lat_tiles
    ne = z.shape[1] * SUBLANES
    fg = jnp.ones((1, d), F32) if final_g is None else final_g
    grid_spec = pltpu.PrefetchScalarGridSpec(
        num_scalar_prefetch=1,
        grid=(bsz, nt_out),
        in_specs=[pl.BlockSpec((1, TILE, d), lambda b, i, t: (b, i, 0)),
                  pl.BlockSpec((1, 1, TILE), lambda b, i, t: (b, 0, i)),
                  pl.BlockSpec((1, 1, TILE), lambda b, i, t: (b, 0, i)),
                  pl.BlockSpec((1, 1, d), lambda b, i, t: (jnp.where(i >= n_lat_tiles, CTX_MOD_ROW, b) * N_MOD + 5, 0, 0)),
                  pl.BlockSpec((1, d), lambda b, i, t: (0, 0)),
                  pl.BlockSpec(memory_space=pl.ANY)],
        out_specs=pl.BlockSpec((1, TILE, d), lambda b, i, t: (b, i, 0)),
        scratch_shapes=[pltpu.VMEM((COMBINE_AHEAD + 1, TILE // SUBLANES, SUBLANES, d // 2), U32),
                        pltpu.SemaphoreType.DMA((COMBINE_AHEAD + 1,)),
                        pltpu.VMEM((TILE, d), F32)],
    )
    return pl.pallas_call(
        functools.partial(_combine_body, n_tiles=nt, ne=ne, final=final_g is not None),
        grid_spec=grid_spec,
        out_shape=jax.ShapeDtypeStruct((bsz, nt_out * TILE, d), F32),
        compiler_params=_cparams(("arbitrary", "arbitrary"), 32),
        name="combine",
    )(ts, x, sx, tc, modl, fg, z)


def _rope_tables(n_lat, n_ctx):
    rows = n_lat // GRID_W
    row, col = jnp.meshgrid(jnp.arange(rows, dtype=F32), jnp.arange(GRID_W, dtype=F32), indexing="ij")
    row, col = row.reshape(-1), col.reshape(-1)
    n_freq = ROPE // 4
    inv_freq = ROPE_BASE ** (-jnp.arange(n_freq, dtype=F32) / n_freq)
    ang = jnp.concatenate([row[:, None] * inv_freq, col[:, None] * inv_freq], axis=-1)
    cos, sin = jnp.cos(ang), jnp.sin(ang)
    half = ROPE // 2
    zero = lambda n: jnp.zeros((n_lat, n), F32)
    rc = jnp.concatenate([cos, cos, jnp.ones((n_lat, LANES - ROPE), F32)], axis=1)
    rsa = jnp.concatenate([-sin, zero(LANES - half)], axis=1)
    rsb = jnp.concatenate([zero(half), sin, zero(LANES - ROPE)], axis=1)
    ident = jnp.ones((n_ctx, LANES), F32)
    none = jnp.zeros((n_ctx, LANES), F32)
    return (jnp.concatenate([rc, ident]), jnp.concatenate([rsa, none]), jnp.concatenate([rsb, none]))


def kernel(x, c, ctx, c_ctx, w_mod, b_mod, norm1_g, norm2_g, ev_w_in, ev_q_g, ev_kv_g, ev_w_uq, ev_w_ukv, ev_conv_w, ev_conv_b, ev_gn_g, ev_gn_b, ev_w_out, od_w_in, od_ln_g, od_ln_b, od_w_s, od_b_s, od_w_out, moe_w_r, moe_w1, moe_w3, moe_w2, final_g):
    bsz, n_lat, d = x.shape
    n_ctx = ctx.shape[1]
    depth = w_mod.shape[0]
    lat_tq = min(LAT_TQ, n_lat)
    assert bsz <= CTX_MOD_ROW and n_lat % lat_tq == 0 and n_ctx == TILE and d % LANES == 0
    n_lat_tiles = n_lat // TILE
    cap_l = CAPACITY_FACTOR * n_lat // N_EXPERTS
    cap_c = CAPACITY_FACTOR * n_ctx // N_EXPERTS
    capt = cap_l + cap_c
    segs = ((0, n_lat, cap_l, 0), (n_lat, n_ctx, cap_c, cap_l))

    cc = jnp.zeros((MOD_ROWS, d), F32).at[:bsz].set(c).at[CTX_MOD_ROW].set(c_ctx)
    mod = _modulation(cc, w_mod, b_mod)
    xs = jnp.concatenate([x, ctx], axis=1)
    rc, rsa, rsb = _rope_tables(n_lat, n_ctx)
    row2 = lambda v: v.reshape(1, -1)
    w1_all, w3_all, w2_all = moe_w1.astype(BF16), moe_w3.astype(BF16), moe_w2.astype(BF16)

    for layer in range(depth):
        i = layer // 2
        modl = mod[layer].reshape(MOD_ROWS * N_MOD, 1, d)
        if layer % 2 == 0:
            w = ev_w_in[i]
            conv_off = Q_RANK + KV_RANK + ROPE
            w_e = jnp.concatenate([w[:, conv_off:], w[:, :conv_off],
                                   jnp.zeros((d, IN_EVEN - w.shape[1]), F32)], axis=1).astype(BF16)
            wuq = jnp.pad(ev_w_uq[i].reshape(Q_RANK, HEADS, NOPE + ROPE),
                          ((0, 0), (0, 0), (0, HEAD_SLOT - NOPE - ROPE))).reshape(Q_RANK, HEADS * HEAD_SLOT).astype(BF16)
            wukv = ev_w_ukv[i].reshape(KV_RANK, HEADS, NOPE + VDIM)
            wuk = wukv[:, :, :NOPE].reshape(KV_RANK, HEADS * NOPE).astype(BF16)
            wuvt = wukv[:, :, NOPE:].reshape(KV_RANK, HEADS * VDIM).T.astype(BF16)
            pc, q, k, vt = _even_in(xs, modl, row2(norm1_g[layer]), w_e, row2(ev_q_g[i]), row2(ev_kv_g[i]),
                                    wuq, wuk, wuvt, rc, rsa, rsb, n_lat_tiles)
            conv_w = jnp.pad(ev_conv_w[i], ((0, 1), (0, 0)))
            c_out = _conv_branch(pc, conv_w, row2(ev_conv_b[i]), row2(ev_gn_g[i]), row2(ev_gn_b[i]), n_lat_tiles)
            a_lat = _attention(q, k, vt, tq=lat_tq, q_tile0=0, n_q_tiles=n_lat // lat_tq, kv_row0=0,
                               n_chunks=n_lat_tiles + 1)
            a_ctx = _attention(q, k, vt, tq=TILE, q_tile0=n_lat_tiles, n_q_tiles=1, kv_row0=n_lat, n_chunks=1)
            a_out = jnp.concatenate([a_lat, a_ctx], axis=1)
            parts = [a_out, c_out]
            w_out = ev_w_out[i].astype(BF16)
        else:
            gated = _odd_in(xs, modl, row2(norm1_g[layer]), od_w_in[i].astype(BF16), row2(od_ln_g[i]),
                            row2(od_ln_b[i]), od_w_s[i].astype(BF16), od_b_s[i].T, n_lat_tiles)
            parts = [gated]
            w_out = od_w_out[i].astype(BF16)
        x_mid, aff_t = _out_proj(parts, w_out, xs, modl, row2(norm2_g[layer]), moe_w_r[layer].T.astype(BF16),
                                 n_lat_tiles)
        sel, sx, tc = _topk(aff_t, segs, capt)
        idx = (sel[:, :, 0] * 256.0 + sel[:, :, 1]).astype(I32)
        pos = (sel[:, :, 2] * 256.0 + sel[:, :, 3]).astype(I32)
        gate = (sel[:, :, 4] + sel[:, :, 5]) + sel[:, :, 6]
        s_all = n_lat + n_ctx
        n_ent = N_EXPERTS * capt
        bofs = jnp.arange(bsz, dtype=I32)[:, None, None]
        ip = jnp.concatenate([bofs * s_all + idx, bofs * n_ent + pos], axis=-1)
        z = _expert_ffn(ip.reshape(bsz * N_EXPERTS, 2 * capt), gate.reshape(bsz * N_EXPERTS, capt, 1),
                        x_mid.reshape(bsz * s_all, d), modl, row2(norm2_g[layer]),
                        w1_all, w3_all, w2_all, layer, bsz, cap_l, capt)
        ts = jnp.concatenate([sx[:, 0, ::TILE], sx[:, 0, -1:] + tc[:, 0, -1:]], axis=1).astype(I32).reshape(-1)
        xs = _combine(ts, x_mid, sx, tc, modl, z.reshape(bsz, n_ent // SUBLANES, SUBLANES, d // 2), n_lat_tiles,
                      final_g=row2(final_g) if layer == depth - 1 else None)
    return xs
```

```python
import functools
import math

import jax
import jax.numpy as jnp
from jax import lax
from jax.experimental import pallas as pl
from jax.experimental.pallas import tpu as pltpu

F32 = jnp.float32
BF16 = jnp.bfloat16
I32 = jnp.int32
U32 = jnp.uint32

NORM_EPS = 1e-6
GRID_W = 64
ROPE_BASE = 10000.0
HEADS = 8
NOPE = 128
ROPE = 64
VDIM = 128
VT_ROWS = 144
Q_RANK = 512
KV_RANK = 256
CONV_CH = 1024
CONV_GROUPS = 8
CONV_WIDTH = 31
GMLP_CH = 2048
GMLP_GROUPS = 8
CHUNK = 128
N_EXPERTS = 16
CAPACITY_FACTOR = 2

LANES = 128
TILE = 256
HALO = 16
HEAD_SLOT = 256
N_MOD = 6
CTX_MOD_ROW = 4
MOD_ROWS = 8
IN_EVEN = 3072
SEL_ROWS = 8
QK_COEF = (NOPE + ROPE) ** -0.5 * math.log2(math.e)
SUBLANES = 8
LAT_TQ = 4096
COMBINE_AHEAD = 2


def _cparams(sem, vmem_mb, flags=None):
    return pltpu.CompilerParams(dimension_semantics=sem, vmem_limit_bytes=vmem_mb << 20, flags=flags)


def _resident(shape):
    nd = len(shape)
    return pl.BlockSpec(shape, lambda *_: (0,) * nd, pipeline_mode=pl.Buffered(1))


def _rms(x, g):
    return x * lax.rsqrt(jnp.mean(x * x, axis=-1, keepdims=True) + NORM_EPS) * g


def _mod_spec(k, n_lat_tiles, d):
    return pl.BlockSpec((1, 1, d), lambda b, i: (jnp.where(i >= n_lat_tiles, CTX_MOD_ROW, b) * N_MOD + k, 0, 0))


def _mod_body(c_ref, w_ref, b_ref, o_ref):
    a = c_ref[...]
    a = a * jax.nn.sigmoid(a)
    o_ref[0] = jnp.dot(a.astype(BF16), w_ref[0].astype(BF16), preferred_element_type=F32) + b_ref[0]


def _modulation(cc, w_mod, b_mod):
    n_layers, d, n = w_mod.shape
    tn = max(t for t in (1024, 512, 256, LANES) if n % t == 0)
    return pl.pallas_call(
        _mod_body,
        grid=(n_layers, n // tn),
        in_specs=[pl.BlockSpec((MOD_ROWS, d), lambda l, j: (0, 0)),
                  pl.BlockSpec((1, d, tn), lambda l, j: (l, 0, j)),
                  pl.BlockSpec((1, 1, tn), lambda l, j: (l, 0, j))],
        out_specs=pl.BlockSpec((1, MOD_ROWS, tn), lambda l, j: (l, 0, j)),
        out_shape=jax.ShapeDtypeStruct((n_layers, MOD_ROWS, n), F32),
        compiler_params=_cparams(("arbitrary", "arbitrary"), 40),
        name="modulation",
    )(cc, w_mod, b_mod.reshape(n_layers, 1, n))


def _rope128(grp, c, sa, sb):
    return grp * c + pltpu.roll(grp, 96, 1) * sa + pltpu.roll(grp, 32, 1) * sb


def _even_in_body(x_ref, sh_ref, sc_ref, g_ref, w_ref, qg_ref, kvg_ref, wuq_ref, wuk_ref, wuvt_ref,
                  rc_ref, rsa_ref, rsb_ref, pc_ref, q_ref, k_ref, vt_ref):
    h = _rms(x_ref[0], g_ref[...]) * (1.0 + sc_ref[0]) + sh_ref[0]
    p = jnp.dot(h.astype(BF16), w_ref[...], preferred_element_type=F32)
    n_conv = 2 * CONV_CH
    pc_ref[0] = p[:, :n_conv].astype(BF16)
    pq = p[:, n_conv:n_conv + Q_RANK]
    pkv = p[:, n_conv + Q_RANK:n_conv + Q_RANK + KV_RANK]
    pkr = p[:, n_conv + Q_RANK + KV_RANK:n_conv + Q_RANK + KV_RANK + LANES]
    rc, rsa, rsb = rc_ref[...], rsa_ref[...], rsb_ref[...]
    q = jnp.dot(_rms(pq, qg_ref[...]).astype(BF16), wuq_ref[...], preferred_element_type=F32) * QK_COEF
    ckv = _rms(pkv, kvg_ref[...]).astype(BF16)
    kn = jnp.dot(ckv, wuk_ref[...], preferred_element_type=F32)
    kr = _rope128(pkr, rc, rsa, rsb).astype(BF16)
    vt = lax.dot_general(wuvt_ref[...], ckv, (((1,), (1,)), ((), ())), preferred_element_type=F32)
    for hd in range(HEADS):
        o = hd * HEAD_SLOT
        q_ref[0, :, o:o + NOPE] = q[:, o:o + NOPE].astype(BF16)
        q_ref[0, :, o + NOPE:o + HEAD_SLOT] = _rope128(q[:, o + NOPE:o + HEAD_SLOT], rc, rsa, rsb).astype(BF16)
        k_ref[0, :, o:o + NOPE] = kn[:, hd * NOPE:(hd + 1) * NOPE].astype(BF16)
        k_ref[0, :, o + NOPE:o + HEAD_SLOT] = kr
        extra = (lax.broadcasted_iota(I32, (VT_ROWS - VDIM, TILE), 0) == 0).astype(F32)
        vt_ref[0, hd, 0] = jnp.concatenate([vt[hd * VDIM:(hd + 1) * VDIM, :], extra], axis=0).astype(BF16)


def _even_in(x, modl, g, w, qg, kvg, wuq, wuk, wuvt, rc, rsa, rsb, n_lat_tiles):
    b, s, d = x.shape
    nt = s // TILE
    tile = lambda n: pl.BlockSpec((1, TILE, n), lambda bb, i: (bb, i, 0))
    rope = pl.BlockSpec((TILE, LANES), lambda bb, i: (i, 0))
    return pl.pallas_call(
        _even_in_body,
        grid=(b, nt),
        in_specs=[tile(d), _mod_spec(0, n_lat_tiles, d), _mod_spec(1, n_lat_tiles, d), _resident((1, d)),
                  _resident(w.shape), _resident(qg.shape), _resident(kvg.shape), _resident(wuq.shape),
                  _resident(wuk.shape), _resident(wuvt.shape), rope, rope, rope],
        out_specs=[tile(2 * CONV_CH), tile(HEADS * HEAD_SLOT), tile(HEADS * HEAD_SLOT),
                   pl.BlockSpec((1, HEADS, 1, VT_ROWS, TILE), lambda bb, i: (bb, 0, i, 0, 0))],
        out_shape=[jax.ShapeDtypeStruct((b, s, 2 * CONV_CH), BF16),
                   jax.ShapeDtypeStruct((b, s, HEADS * HEAD_SLOT), BF16),
                   jax.ShapeDtypeStruct((b, s, HEADS * HEAD_SLOT), BF16),
                   jax.ShapeDtypeStruct((b, HEADS, nt, VT_ROWS, TILE), BF16)],
        compiler_params=_cparams(("arbitrary", "arbitrary"), 48),
        name="even_in",
    )(x, modl, modl, g, w, qg, kvg, wuq, wuk, wuvt, rc, rsa, rsb)


def _glu(v):
    return v[:, :CONV_CH].astype(F32) * jax.nn.sigmoid(v[:, CONV_CH:].astype(F32))


def _conv_body(cur_ref, prev_ref, next_ref, w_ref, b_ref, gg_ref, gb_ref, o_ref, ubuf, shift_s,
               *, n_lat_tiles, n_tiles):
    i = pl.program_id(1)
    left_ok = jnp.logical_and(i != 0, i != n_lat_tiles)
    right_ok = jnp.logical_and(i != n_lat_tiles - 1, i != n_tiles - 1)
    ubuf[0:HALO, :] = jnp.where(left_ok, _glu(prev_ref[0]), 0.0)
    ubuf[HALO:HALO + TILE, :] = _glu(cur_ref[0])
    ubuf[HALO + TILE:2 * HALO + TILE, :] = jnp.where(right_ok, _glu(next_ref[0]), 0.0)
    pad = CONV_WIDTH // 2
    gsz = CONV_CH // CONV_GROUPS
    rows = 64
    sh_rows = shift_s.shape[1]
    for grp in range(CONV_GROUPS):
        ls = slice(grp * gsz, (grp + 1) * gsz)
        for sft in range(1, SUBLANES):
            shift_s[sft - 1] = ubuf[sft:sft + sh_rows, ls]
        for r0 in range(0, TILE, rows):
            acc = jnp.zeros((rows, gsz), F32)
            for kk in range(CONV_WIDTH):
                blk, sft = divmod(HALO - pad + kk, SUBLANES)
                o = blk * SUBLANES + r0
                src = ubuf[o:o + rows, ls] if sft == 0 else shift_s[sft - 1, o:o + rows, :]
                acc = acc + w_ref[kk:kk + 1, ls] * src
            y = acc + b_ref[:, ls]
            mu = jnp.mean(y, axis=-1, keepdims=True)
            yc = y - mu
            var = jnp.mean(yc * yc, axis=-1, keepdims=True)
            yn = yc * lax.rsqrt(var + NORM_EPS) * gg_ref[:, ls] + gb_ref[:, ls]
            o_ref[0, r0:r0 + rows, ls] = (yn * jax.nn.sigmoid(yn)).astype(BF16)


def _conv_branch(pc, w, bias, gg, gb, n_lat_tiles):
    b, s, _ = pc.shape
    nt = s // TILE
    hb = TILE // HALO
    nh = s // HALO
    return pl.pallas_call(
        functools.partial(_conv_body, n_lat_tiles=n_lat_tiles, n_tiles=nt),
        grid=(b, nt),
        in_specs=[pl.BlockSpec((1, TILE, 2 * CONV_CH), lambda bb, i: (bb, i, 0)),
                  pl.BlockSpec((1, HALO, 2 * CONV_CH), lambda bb, i: (bb, jnp.maximum(i * hb - 1, 0), 0)),
                  pl.BlockSpec((1, HALO, 2 * CONV_CH), lambda bb, i: (bb, jnp.minimum((i + 1) * hb, nh - 1), 0)),
                  _resident(w.shape), _resident(bias.shape), _resident(gg.shape), _resident(gb.shape)],
        out_specs=pl.BlockSpec((1, TILE, CONV_CH), lambda bb, i: (bb, i, 0)),
        out_shape=jax.ShapeDtypeStruct((b, s, CONV_CH), BF16),
        scratch_shapes=[pltpu.VMEM((TILE + 2 * HALO, CONV_CH), F32),
                        pltpu.VMEM((SUBLANES - 1, TILE + 2 * HALO - SUBLANES, CONV_CH // CONV_GROUPS), F32)],
        compiler_params=_cparams(("arbitrary", "arbitrary"), 32),
        name="conv_branch",
    )(pc, pc, pc, w, bias, gg, gb)


ATTN_RING = 4


def _attn_body(q_ref, k_ref, v_ref, o_ref, *scratch, n_chunks):
    bufs = tuple(scratch[4 * r:4 * r + 4] for r in range(ATTN_RING))
    m_s, acc_s = scratch[4 * ATTN_RING:]
    q = q_ref[0]
    m_s[...] = jnp.full(m_s.shape, -jnp.inf, F32)
    acc_s[...] = jnp.zeros(acc_s.shape, F32)
    ring = len(bufs)
    lag = ring // 2

    def scores(j, buf):
        kc = k_ref[0, pl.ds(pl.multiple_of(j * TILE, TILE), TILE), :]
        st = lax.dot_general(kc, q, (((1,), (1,)), ((), ())), preferred_element_type=F32)
        buf[0][...] = st
        buf[3][...] = jnp.max(st, axis=0, keepdims=True)

    def softmax(buf):
        st = buf[0][...]
        m_old = m_s[...]
        m_new = jnp.maximum(m_old, buf[3][...])
        alpha = jnp.exp2(m_old - m_new)
        p = jnp.exp2(st - m_new)
        m_s[...] = m_new
        buf[1][...] = p.astype(BF16)
        buf[2][...] = alpha

    def weighted_values(j, buf):
        acc_s[...] = buf[2][...] * acc_s[...] + jnp.dot(v_ref[0, 0, j], buf[1][...], preferred_element_type=F32)

    def stage(t, t_static):
        if 0 <= t_static + lag < n_chunks or not isinstance(t, int):
            scores(t + lag, bufs[(t_static + lag) % ring])
        if 0 <= t_static < n_chunks or not isinstance(t, int):
            softmax(bufs[t_static % ring])
        if 0 <= t_static - lag < n_chunks or not isinstance(t, int):
            weighted_values(t - lag, bufs[(t_static - lag) % ring])

    t_full0 = lag
    n_full = max(n_chunks - 2 * lag, 0)
    n_body = n_full // ring
    for t in range(-lag, min(t_full0, n_chunks + lag)):
        stage(t, t)

    def body(i, carry):
        t0 = t_full0 + ring * i
        for r in range(ring):
            stage(t0 + r, t_full0 + r)
        return carry

    lax.fori_loop(0, n_body, body, 0)
    for t in range(t_full0 + n_body * ring, n_chunks + lag):
        if t >= t_full0:
            stage(t, t)
    o_ref[0] = (acc_s[0:VDIM, :] / acc_s[VDIM:VDIM + 1, :]).T.astype(BF16)


def _attention(q, k, vt, *, tq, q_tile0, n_q_tiles, kv_row0, n_chunks):
    b, s, _ = q.shape
    kv_rows = n_chunks * TILE
    kb = kv_row0 // kv_rows
    cb = kv_row0 // TILE // n_chunks
    return pl.pallas_call(
        functools.partial(_attn_body, n_chunks=n_chunks),
        grid=(b, HEADS, n_q_tiles),
        in_specs=[pl.BlockSpec((1, tq, HEAD_SLOT), lambda bb, h, i: (bb, q_tile0 + i, h)),
                  pl.BlockSpec((1, kv_rows, HEAD_SLOT), lambda bb, h, i: (bb, kb, h)),
                  pl.BlockSpec((1, 1, n_chunks, VT_ROWS, TILE), lambda bb, h, i: (bb, h, cb, 0, 0))],
        out_specs=pl.BlockSpec((1, tq, VDIM), lambda bb, h, i: (bb, i, h)),
        out_shape=jax.ShapeDtypeStruct((b, n_q_tiles * tq, HEADS * VDIM), BF16),
        scratch_shapes=[pltpu.VMEM((TILE, tq), F32), pltpu.VMEM((TILE, tq), BF16),
                        pltpu.VMEM((1, tq), F32), pltpu.VMEM((1, tq), F32)] * ATTN_RING
        + [pltpu.VMEM((1, tq), F32), pltpu.VMEM((VT_ROWS, tq), F32)],
        compiler_params=_cparams(("arbitrary", "arbitrary", "arbitrary"), 58),
        name="attention",
    )(q, k, vt)


def _gelu_tanh(x):
    return 0.5 * x * (1.0 + jnp.tanh(math.sqrt(2.0 / math.pi) * (x + 0.044715 * (x * x * x))))


def _odd_in_body(x_ref, sh_ref, sc_ref, g_ref, w_ref, lg_ref, lb_ref, ws_ref, bst_ref, o_ref):
    h = _rms(x_ref[0], g_ref[...]) * (1.0 + sc_ref[0]) + sh_ref[0]
    z = _gelu_tanh(jnp.dot(h.astype(BF16), w_ref[...], preferred_element_type=F32))
    u = z[:, :GMLP_CH]
    v = z[:, GMLP_CH:]
    mu = jnp.mean(v, axis=-1, keepdims=True)
    vc = v - mu
    var = jnp.mean(vc * vc, axis=-1, keepdims=True)
    vn = (vc * lax.rsqrt(var + NORM_EPS) * lg_ref[...] + lb_ref[...]).astype(BF16)
    gsz = GMLP_CH // GMLP_GROUPS
    for c0 in range(0, TILE, CHUNK):
        for grp in range(GMLP_GROUPS):
            ls = slice(grp * gsz, (grp + 1) * gsz)
            sg = jnp.dot(ws_ref[grp], vn[c0:c0 + CHUNK, ls], preferred_element_type=F32) + bst_ref[:, grp:grp + 1]
            o_ref[0, c0:c0 + CHUNK, ls] = (u[c0:c0 + CHUNK, ls] * sg).astype(BF16)


def _odd_in(x, modl, g, w, lg, lb, ws, bst, n_lat_tiles):
    b, s, d = x.shape
    nt = s // TILE
    return pl.pallas_call(
        _odd_in_body,
        grid=(b, nt),
        in_specs=[pl.BlockSpec((1, TILE, d), lambda bb, i: (bb, i, 0)),
                  _mod_spec(0, n_lat_tiles, d), _mod_spec(1, n_lat_tiles, d), _resident((1, d)),
                  _resident(w.shape), _resident(lg.shape), _resident(lb.shape), _resident(ws.shape),
                  _resident(bst.shape)],
        out_specs=pl.BlockSpec((1, TILE, GMLP_CH), lambda bb, i: (bb, i, 0)),
        out_shape=jax.ShapeDtypeStruct((b, s, GMLP_CH), BF16),
        compiler_params=_cparams(("arbitrary", "arbitrary"), 56),
        name="odd_in",
    )(x, modl, modl, g, w, lg, lb, ws, bst)


def _out_body(*refs, n_parts):
    a_refs = refs[:n_parts]
    w_ref, x_ref, g1_ref, n2_ref, sh_ref, sc_ref, wrt_ref, xo_ref, aff_ref = refs[n_parts:]
    y = None
    off = 0
    for a_ref in a_refs:
        kk = a_ref.shape[-1]
        t = jnp.dot(a_ref[0], w_ref[off:off + kk, :], preferred_element_type=F32)
        y = t if y is None else y + t
        off += kk
    xn = x_ref[0] + g1_ref[0] * y
    xo_ref[0] = xn
    h2 = _rms(xn, n2_ref[...]) * (1.0 + sc_ref[0]) + sh_ref[0]
    lg = lax.dot_general(wrt_ref[...], h2.astype(BF16), (((1,), (1,)), ((), ())), preferred_element_type=F32)
    e = jnp.exp(lg - jnp.max(lg, axis=0, keepdims=True))
    aff_ref[0] = e / jnp.sum(e, axis=0, keepdims=True)


def _out_proj(parts, w, x, modl, n2, wrt, n_lat_tiles):
    b, s, d = x.shape
    nt = s // TILE
    tile = lambda n: pl.BlockSpec((1, TILE, n), lambda bb, i: (bb, i, 0))
    return pl.pallas_call(
        functools.partial(_out_body, n_parts=len(parts)),
        grid=(b, nt),
        in_specs=[tile(a.shape[-1]) for a in parts]
        + [_resident(w.shape), tile(d), _mod_spec(2, n_lat_tiles, d), _resident((1, d)),
           _mod_spec(3, n_lat_tiles, d), _mod_spec(4, n_lat_tiles, d), _resident(wrt.shape)],
        out_specs=[tile(d), pl.BlockSpec((1, N_EXPERTS, TILE), lambda bb, i: (bb, 0, i))],
        out_shape=[jax.ShapeDtypeStruct((b, s, d), F32), jax.ShapeDtypeStruct((b, N_EXPERTS, s), F32)],
        compiler_params=_cparams(("arbitrary", "arbitrary"), 40),
        name="out_proj",
    )(*parts, w, x, modl, n2, modl, modl, wrt)


def _count(pred):
    return jnp.sum(pred.astype(F32), axis=1, keepdims=True)


def _cumsum_lanes(x):
    r, n = x.shape
    tri = (lax.broadcasted_iota(I32, (LANES, LANES), 0) <= lax.broadcasted_iota(I32, (LANES, LANES), 1)).astype(BF16)
    outs = []
    off = jnp.zeros((r, 1), F32)
    for blk in range(n // LANES):
        loc = jnp.dot(x[:, blk * LANES:(blk + 1) * LANES].astype(BF16), tri, preferred_element_type=F32) + off
        outs.append(loc)
        off = loc[:, LANES - 1:LANES]
    return jnp.concatenate(outs, axis=1)


def _split3(x):
    hi = x.astype(BF16).astype(F32)
    r1 = x - hi
    mid = r1.astype(BF16).astype(F32)
    lo = (r1 - mid).astype(BF16).astype(F32)
    return hi, mid, lo


def _topk_body(aff_ref, sel_ref, sx_ref, tc_ref, mask_s, pm_s, planes_s, res_s, *, segs):
    s_total = aff_ref.shape[-1]
    for off, n, cap, _ in segs:
        a = aff_ref[0, :, off:off + n]
        bits = pltpu.bitcast(a, I32)
        thr = jnp.zeros((N_EXPERTS, 1), I32)
        for bit in range(30, -1, -1):
            cand = thr | (1 << bit)
            thr = jnp.where(_count(bits >= cand) >= cap, cand, thr)
        gt = bits > thr
        eq = bits == thr
        need = cap - _count(gt)
        idx = lax.broadcasted_iota(I32, (N_EXPERTS, n), 1)
        j0 = jnp.zeros((N_EXPERTS, 1), I32)
        for bit in range(n.bit_length() - 1, -1, -1):
            cand = j0 | (1 << bit)
            j0 = jnp.where(_count(jnp.logical_and(eq, idx < cand)) < need, cand, j0)
        m = jnp.logical_or(gt, jnp.logical_and(eq, idx <= j0)).astype(F32)
        mask_s[:, off:off + n] = m
        pm_s[:, off:off + n] = _cumsum_lanes(m) * m

    m_all = mask_s[...]
    tc = jnp.sum(m_all, axis=0, keepdims=True)
    sx = _cumsum_lanes(jnp.broadcast_to(tc, (8, s_total)))[0:1] - tc
    tc_ref[0] = tc
    sx_ref[0] = sx
    lower = (lax.broadcasted_iota(I32, (N_EXPERTS, N_EXPERTS), 1)
             < lax.broadcasted_iota(I32, (N_EXPERTS, N_EXPERTS), 0)).astype(BF16)
    rank = jnp.dot(lower, m_all.astype(BF16), preferred_element_type=F32)
    pos = sx + rank
    tok = lax.broadcasted_iota(I32, (N_EXPERTS, s_total), 1).astype(F32)
    g_hi, g_mid, g_lo = _split3(aff_ref[0])
    tok_hi = jnp.floor(tok * (1.0 / 256.0))
    pos_hi = jnp.floor(pos * (1.0 / 256.0))
    planes = (tok_hi, tok - 256.0 * tok_hi, pos_hi, pos - 256.0 * pos_hi, g_hi, g_mid, g_lo, jnp.zeros_like(tok))
    for r, pln in enumerate(planes):
        planes_s[r * N_EXPERTS:(r + 1) * N_EXPERTS, :] = pln.astype(BF16)

    def per_expert(e, carry):
        for off, n, cap, slot0 in segs:
            blk = min(cap, TILE)
            prow = pm_s[pl.ds(e, 1), off:off + n]
            vals = planes_s[:, off:off + n]
            for c0 in range(0, cap, blk):
                slot = (lax.broadcasted_iota(I32, (blk, n), 0) + (c0 + 1)).astype(F32)
                onehot = (prow == slot).astype(BF16)
                res_s[:, 0:blk] = lax.dot_general(vals, onehot, (((1,), (1,)), ((), ())),
                                                  preferred_element_type=F32)
                for r in range(SEL_ROWS):
                    row = res_s[pl.ds(r * N_EXPERTS + e, 1), :]
                    sel_ref[0, e, r:r + 1, slot0 + c0:slot0 + c0 + blk] = row[:, 0:blk]
        return carry

    lax.fori_loop(0, N_EXPERTS, per_expert, 0)


def _topk(aff_t, segs, capt):
    b, e, s = aff_t.shape
    return pl.pallas_call(
        functools.partial(_topk_body, segs=segs),
        grid=(b,),
        in_specs=[pl.BlockSpec((1, e, s), lambda bb: (bb, 0, 0))],
        out_specs=[pl.BlockSpec((1, e, SEL_ROWS, capt), lambda bb: (bb, 0, 0, 0)),
                   pl.BlockSpec((1, 1, s), lambda bb: (bb, 0, 0)),
                   pl.BlockSpec((1, 1, s), lambda bb: (bb, 0, 0))],
        out_shape=[jax.ShapeDtypeStruct((b, e, SEL_ROWS, capt), F32),
                   jax.ShapeDtypeStruct((b, 1, s), F32), jax.ShapeDtypeStruct((b, 1, s), F32)],
        scratch_shapes=[pltpu.VMEM((e, s), F32), pltpu.VMEM((e, s), F32), pltpu.VMEM((SEL_ROWS * e, s), BF16),
                        pltpu.VMEM((SEL_ROWS * e, TILE), F32)],
        compiler_params=_cparams(("arbitrary",), 48),
        name="topk",
    )(aff_t)


def _pack_bf16_pairs(y):
    half = y.shape[1] // 2
    hi = pltpu.bitcast(y[:, :half].astype(BF16).astype(F32), U32)
    lo = pltpu.bitcast(y[:, half:].astype(BF16).astype(F32), U32)
    return hi | lax.shift_right_logical(lo, jnp.uint32(16))


def _unpack_bf16_pairs(w):
    hi = pltpu.bitcast(w & jnp.uint32(0xFFFF0000), F32).astype(BF16)
    lo = pltpu.bitcast(lax.shift_left(w, jnp.uint32(16)), F32).astype(BF16)
    return hi, lo


def _expert_body(ip_hbm, gate0_ref, gate1_ref, x_hbm, sh0_ref, sc0_ref, sh1_ref, sc1_ref, shc_ref, scc_ref, n2_ref,
                 w1_ref, w3_ref, w2_ref, z_hbm, ip0, ip1, ip_sem, xs_buf, gsem, y_buf, ssem, *, n_b, cap_l, capt):
    e = pl.program_id(0)
    pb = pl.program_id(1)
    n_e = pl.num_programs(0)
    n_pb = n_b // 2
    s0 = (e * n_pb + pb) * 2
    n_steps = n_e * n_b
    b0 = 2 * pb
    wrap = pb + 1 == n_pb
    e2 = jnp.where(wrap, e + 1, e)
    b2 = jnp.where(wrap, 0, b0 + 2)
    ips = (ip0, ip1)
    n_grp = capt // SUBLANES

    def ip_copy(ee, bb, u):
        return pltpu.make_async_copy(ip_hbm.at[bb * n_e + ee], ips[u], ip_sem.at[u])

    def gather_start(u):
        for c in range(capt):
            pltpu.make_async_copy(x_hbm.at[pl.ds(ips[u][c], 1)], xs_buf.at[u, pl.ds(c, 1)], gsem.at[u]).start()

    def scatter_start(u):
        for c in range(capt):
            pltpu.make_async_copy(y_buf.at[pl.ds(c, 1)], z_hbm.at[pl.ds(ips[u][capt + c], 1)],
                                  ssem).start(priority=c % 2)

    def scatter_wait():
        pltpu.make_async_copy(y_buf, z_hbm.at[pl.ds(0, capt)], ssem).wait()

    def gather_wait(u):
        pltpu.make_async_copy(x_hbm.at[pl.ds(0, capt)], xs_buf.at[u], gsem.at[u]).wait()

    def compute(u, gate_ref, sh_ref, sc_ref):
        xs = xs_buf[u]
        hn = _rms(xs, n2_ref[...])
        is_lat = lax.broadcasted_iota(I32, (capt, 1), 0) < cap_l
        sc = jnp.where(is_lat, sc_ref[0], scc_ref[0])
        sh = jnp.where(is_lat, sh_ref[0], shc_ref[0])
        hb = (hn * (1.0 + sc) + sh).astype(BF16)
        a1 = jnp.dot(hb, w1_ref[0, 0], preferred_element_type=F32)
        a3 = jnp.dot(hb, w3_ref[0, 0], preferred_element_type=F32)
        hid = (a1 * jax.nn.sigmoid(a1) * a3).astype(BF16)
        y = jnp.dot(hid, w2_ref[0, 0], preferred_element_type=F32) * gate_ref[0]
        return _pack_bf16_pairs(y)

    @pl.when(s0 == 0)
    def _():
        ip_copy(e, b0, 0).start()
        ip_copy(e, b0, 0).wait()
        gather_start(0)
        ip_copy(e, b0 + 1, 1).start()

    always = s0 >= 0
    ip_copy(e, b0 + 1, 1).wait()

    @pl.when(always)
    def _():
        gather_start(1)

    gather_wait(0)
    packed = compute(0, gate0_ref, sh0_ref, sc0_ref)

    @pl.when(s0 > 0)
    def _():
        scatter_wait()

    y_buf[...] = packed

    @pl.when(always)
    def _():
        scatter_start(0)

    @pl.when(s0 + 2 < n_steps)
    def _():
        ip_copy(e2, b2, 0).start()

    @pl.when(s0 + 2 < n_steps)
    def _():
        ip_copy(e2, b2, 0).wait()
        gather_start(0)

    gather_wait(1)
    packed = compute(1, gate1_ref, sh1_ref, sc1_ref)
    scatter_wait()
    y_buf[...] = packed

    @pl.when(always)
    def _():
        scatter_start(1)

    @pl.when(s0 + 3 < n_steps)
    def _():
        ip_copy(e2, b2 + 1, 1).start()

    @pl.when(s0 + 2 == n_steps)
    def _():
        scatter_wait()


def _expert_ffn(ip, gate, x2, modl, n2, w1, w3, w2, layer, bsz, cap_l, capt):
    _, d = x2.shape
    _, n_e, _, f = w1.shape
    assert bsz % 2 == 0 and capt % SUBLANES == 0
    ne = n_e * capt
    any_spec = pl.BlockSpec(memory_space=pl.ANY)
    gate_spec = lambda u: pl.BlockSpec((1, capt, 1), lambda e, pb: ((2 * pb + u) * n_e + e, 0, 0))
    mod_b = lambda u, k: pl.BlockSpec((1, 1, d), lambda e, pb: ((2 * pb + u) * N_MOD + k, 0, 0))
    mod_c = lambda k: pl.BlockSpec((1, 1, d), lambda e, pb: (CTX_MOD_ROW * N_MOD + k, 0, 0))
    return pl.pallas_call(
        functools.partial(_expert_body, n_b=bsz, cap_l=cap_l, capt=capt),
        grid=(n_e, bsz // 2),
        in_specs=[any_spec, gate_spec(0), gate_spec(1), any_spec,
                  mod_b(0, 3), mod_b(0, 4), mod_b(1, 3), mod_b(1, 4), mod_c(3), mod_c(4), _resident((1, d)),
                  pl.BlockSpec((1, 1, d, f), lambda e, pb: (layer, e, 0, 0)),
                  pl.BlockSpec((1, 1, d, f), lambda e, pb: (layer, e, 0, 0)),
                  pl.BlockSpec((1, 1, f, d), lambda e, pb: (layer, e, 0, 0))],
        out_specs=any_spec,
        out_shape=jax.ShapeDtypeStruct((bsz * ne, d // 2), U32),
        scratch_shapes=[pltpu.SMEM((2 * capt,), I32), pltpu.SMEM((2 * capt,), I32), pltpu.SemaphoreType.DMA((2,)),
                        pltpu.VMEM((2, capt, d), F32), pltpu.SemaphoreType.DMA((2,)),
                        pltpu.VMEM((capt, d // 2), U32), pltpu.SemaphoreType.DMA(())],
        compiler_params=_cparams(("arbitrary", "arbitrary"), 60),
        name="expert_ffn",
    )(ip, gate, gate, x2, modl, modl, modl, modl, modl, modl, n2, w1, w3, w2)


def _combine_body(ts_ref, x_ref, sx_ref, tc_ref, g2_ref, fg_ref, z_hbm, o_ref, zbuf, zsem, acc_s,
                  *, n_tiles, ne, final):
    b = pl.program_id(0)
    i = pl.program_id(1)
    n_b = pl.num_programs(0)
    half = acc_s.shape[1] // 2
    grp = TILE // SUBLANES

    def tile_range(bb, ii):
        e_lo = ts_ref[bb * (n_tiles + 1) + ii]
        e_hi = ts_ref[bb * (n_tiles + 1) + ii + 1]
        base = lax.shift_left(lax.shift_right_logical(e_lo, 3), 3)
        return base, lax.shift_right_logical(e_hi - base + (TILE - 1), TILE.bit_length() - 1)

    def chunk_start(base, k):
        return jnp.minimum(base + k * TILE, ne - TILE)

    def chunk_copy(bb, base, k, sl):
        g0 = lax.shift_right_logical(chunk_start(base, k), 3)
        return pltpu.make_async_copy(z_hbm.at[bb, pl.ds(g0, grp)], zbuf.at[sl], zsem.at[sl])

    base, n_ch = tile_range(b, i)
    acc_s[...] = jnp.zeros(acc_s.shape, F32)
    sx = sx_ref[0]
    end = sx + tc_ref[0]

    def start_head(bb, hbase, hn_ch):
        for k in range(COMBINE_AHEAD):
            @pl.when(hn_ch > k)
            def _():
                chunk_copy(bb, hbase, k, k).start()

    @pl.when(jnp.logical_and(b == 0, i == 0))
    def _():
        start_head(b, base, n_ch)

    def body(k, carry):
        sl = lax.rem(k, COMBINE_AHEAD + 1)

        @pl.when(k + COMBINE_AHEAD < n_ch)
        def _():
            chunk_copy(b, base, k + COMBINE_AHEAD, lax.rem(k + COMBINE_AHEAD, COMBINE_AHEAD + 1)).start()

        chunk_copy(b, base, k, sl).wait()
        ids = (chunk_start(base, k) + lax.broadcasted_iota(I32, (TILE, 1), 0))
        idf = ids.astype(F32)
        own = jnp.logical_and(jnp.logical_and(ids >= base + k * TILE, idf >= sx), idf < end)
        own = own.astype(BF16)
        hi, lo = _unpack_bf16_pairs(zbuf[sl].reshape(TILE, half))
        dims = (((0,), (0,)), ((), ()))
        acc_s[:, :half] += lax.dot_general(own, hi, dims, preferred_element_type=F32)
        acc_s[:, half:] += lax.dot_general(own, lo, dims, preferred_element_type=F32)
        return carry

    lax.fori_loop(0, n_ch, body, 0)

    last_tile = i + 1 == pl.num_programs(1)
    nb = jnp.where(last_tile, b + 1, b)
    ni = jnp.where(last_tile, 0, i + 1)

    @pl.when(nb < n_b)
    def _():
        nbase, nn_ch = tile_range(nb, ni)
        start_head(nb, nbase, nn_ch)

    out = x_ref[0] + g2_ref[0] * acc_s[...]
    o_ref[0] = _rms(out, fg_ref[...]) if final else out


def _combine(ts, x, sx, tc, modl, z, n_lat_tiles, final_g=None):
    bsz, s, d = x.shape
    nt = s // TILE
    nt_out = nt if final_g is None else n_lat_tiles
    ne = z.shape[1] * SUBLANES
    fg = jnp.ones((1, d), F32) if final_g is None else final_g
    grid_spec = pltpu.PrefetchScalarGridSpec(
        num_scalar_prefetch=1,
        grid=(bsz, nt_out),
        in_specs=[pl.BlockSpec((1, TILE, d), lambda b, i, t: (b, i, 0)),
                  pl.BlockSpec((1, 1, TILE), lambda b, i, t: (b, 0, i)),
                  pl.BlockSpec((1, 1, TILE), lambda b, i, t: (b, 0, i)),
                  pl.BlockSpec((1, 1, d), lambda b, i, t: (jnp.where(i >= n_lat_tiles, CTX_MOD_ROW, b) * N_MOD + 5, 0, 0)),
                  pl.BlockSpec((1, d), lambda b, i, t: (0, 0)),
                  pl.BlockSpec(memory_space=pl.ANY)],
        out_specs=pl.BlockSpec((1, TILE, d), lambda b, i, t: (b, i, 0)),
        scratch_shapes=[pltpu.VMEM((COMBINE_AHEAD + 1, TILE // SUBLANES, SUBLANES, d // 2), U32),
                        pltpu.SemaphoreType.DMA((COMBINE_AHEAD + 1,)),
                        pltpu.VMEM((TILE, d), F32)],
    )
    return pl.pallas_call(
        functools.partial(_combine_body, n_tiles=nt, ne=ne, final=final_g is not None),
        grid_spec=grid_spec,
        out_shape=jax.ShapeDtypeStruct((bsz, nt_out * TILE, d), F32),
        compiler_params=_cparams(("arbitrary", "arbitrary"), 32),
        name="combine",
    )(ts, x, sx, tc, modl, fg, z)


def _rope_tables(n_lat, n_ctx):
    rows = n_lat // GRID_W
    row, col = jnp.meshgrid(jnp.arange(rows, dtype=F32), jnp.arange(GRID_W, dtype=F32), indexing="ij")
    row, col = row.reshape(-1), col.reshape(-1)
    n_freq = ROPE // 4
    inv_freq = ROPE_BASE ** (-jnp.arange(n_freq, dtype=F32) / n_freq)
    ang = jnp.concatenate([row[:, None] * inv_freq, col[:, None] * inv_freq], axis=-1)
    cos, sin = jnp.cos(ang), jnp.sin(ang)
    half = ROPE // 2
    zero = lambda n: jnp.zeros((n_lat, n), F32)
    rc = jnp.concatenate([cos, cos, jnp.ones((n_lat, LANES - ROPE), F32)], axis=1)
    rsa = jnp.concatenate([-sin, zero(LANES - half)], axis=1)
    rsb = jnp.concatenate([zero(half), sin, zero(LANES - ROPE)], axis=1)
    ident = jnp.ones((n_ctx, LANES), F32)
    none = jnp.zeros((n_ctx, LANES), F32)
    return (jnp.concatenate([rc, ident]), jnp.concatenate([rsa, none]), jnp.concatenate([rsb, none]))


def kernel(x, c, ctx, c_ctx, w_mod, b_mod, norm1_g, norm2_g, ev_w_in, ev_q_g, ev_kv_g, ev_w_uq, ev_w_ukv, ev_conv_w, ev_conv_b, ev_gn_g, ev_gn_b, ev_w_out, od_w_in, od_ln_g, od_ln_b, od_w_s, od_b_s, od_w_out, moe_w_r, moe_w1, moe_w3, moe_w2, final_g):
    bsz, n_lat, d = x.shape
    n_ctx = ctx.shape[1]
    depth = w_mod.shape[0]
    lat_tq = min(LAT_TQ, n_lat)
    assert bsz <= CTX_MOD_ROW and n_lat % lat_tq == 0 and n_ctx == TILE and d % LANES == 0
    n_lat_tiles = n_lat // TILE
    cap_l = CAPACITY_FACTOR * n_lat // N_EXPERTS
    cap_c = CAPACITY_FACTOR * n_ctx // N_EXPERTS
    capt = cap_l + cap_c
    segs = ((0, n_lat, cap_l, 0), (n_lat, n_ctx, cap_c, cap_l))

    cc = jnp.zeros((MOD_ROWS, d), F32).at[:bsz].set(c).at[CTX_MOD_ROW].set(c_ctx)
    mod = _modulation(cc, w_mod, b_mod)
    xs = jnp.concatenate([x, ctx], axis=1)
    rc, rsa, rsb = _rope_tables(n_lat, n_ctx)
    row2 = lambda v: v.reshape(1, -1)
    w1_all, w3_all, w2_all = moe_w1.astype(BF16), moe_w3.astype(BF16), moe_w2.astype(BF16)

    for layer in range(depth):
        i = layer // 2
        modl = mod[layer].reshape(MOD_ROWS * N_MOD, 1, d)
        if layer % 2 == 0:
            w = ev_w_in[i]
            conv_off = Q_RANK + KV_RANK + ROPE
            w_e = jnp.concatenate([w[:, conv_off:], w[:, :conv_off],
                                   jnp.zeros((d, IN_EVEN - w.shape[1]), F32)], axis=1).astype(BF16)
            wuq = jnp.pad(ev_w_uq[i].reshape(Q_RANK, HEADS, NOPE + ROPE),
                          ((0, 0), (0, 0), (0, HEAD_SLOT - NOPE - ROPE))).reshape(Q_RANK, HEADS * HEAD_SLOT).astype(BF16)
            wukv = ev_w_ukv[i].reshape(KV_RANK, HEADS, NOPE + VDIM)
            wuk = wukv[:, :, :NOPE].reshape(KV_RANK, HEADS * NOPE).astype(BF16)
            wuvt = wukv[:, :, NOPE:].reshape(KV_RANK, HEADS * VDIM).T.astype(BF16)
            pc, q, k, vt = _even_in(xs, modl, row2(norm1_g[layer]), w_e, row2(ev_q_g[i]), row2(ev_kv_g[i]),
                                    wuq, wuk, wuvt, rc, rsa, rsb, n_lat_tiles)
            conv_w = jnp.pad(ev_conv_w[i], ((0, 1), (0, 0)))
            c_out = _conv_branch(pc, conv_w, row2(ev_conv_b[i]), row2(ev_gn_g[i]), row2(ev_gn_b[i]), n_lat_tiles)
            a_lat = _attention(q, k, vt, tq=lat_tq, q_tile0=0, n_q_tiles=n_lat // lat_tq, kv_row0=0,
                               n_chunks=n_lat_tiles + 1)
            a_ctx = _attention(q, k, vt, tq=TILE, q_tile0=n_lat_tiles, n_q_tiles=1, kv_row0=n_lat, n_chunks=1)
            a_out = jnp.concatenate([a_lat, a_ctx], axis=1)
            parts = [a_out, c_out]
            w_out = ev_w_out[i].astype(BF16)
        else:
            gated = _odd_in(xs, modl, row2(norm1_g[layer]), od_w_in[i].astype(BF16), row2(od_ln_g[i]),
                            row2(od_ln_b[i]), od_w_s[i].astype(BF16), od_b_s[i].T, n_lat_tiles)
            parts = [gated]
            w_out = od_w_out[i].astype(BF16)
        x_mid, aff_t = _out_proj(parts, w_out, xs, modl, row2(norm2_g[layer]), moe_w_r[layer].T.astype(BF16),
                                 n_lat_tiles)
        sel, sx, tc = _topk(aff_t, segs, capt)
        idx = (sel[:, :, 0] * 256.0 + sel[:, :, 1]).astype(I32)
        pos = (sel[:, :, 2] * 256.0 + sel[:, :, 3]).astype(I32)
        gate = (sel[:, :, 4] + sel[:, :, 5]) + sel[:, :, 6]
        s_all = n_lat + n_ctx
        n_ent = N_EXPERTS * capt
        bofs = jnp.arange(bsz, dtype=I32)[:, None, None]
        ip = jnp.concatenate([bofs * s_all + idx, bofs * n_ent + pos], axis=-1)
        z = _expert_ffn(ip.reshape(bsz * N_EXPERTS, 2 * capt), gate.reshape(bsz * N_EXPERTS, capt, 1),
                        x_mid.reshape(bsz * s_all, d), modl, row2(norm2_g[layer]),
                        w1_all, w3_all, w2_all, layer, bsz, cap_l, capt)
        ts = jnp.concatenate([sx[:, 0, ::TILE], sx[:, 0, -1:] + tc[:, 0, -1:]], axis=1).astype(I32).reshape(-1)
        xs = _combine(ts, x_mid, sx, tc, modl, z.reshape(bsz, n_ent // SUBLANES, SUBLANES, d // 2), n_lat_tiles,
                      final_g=row2(final_g) if layer == depth - 1 else None)
    return xs
```
